```python
import jax, jax.numpy as jnp
from jax import lax
import numpy as np

D_MODEL = 1024
BATCH = 1
SEQ = 16384
DEPTH = 1
DEC_BATCH = 32
DEC_SEQ = 8
PAST_LEN = 16384
PAGE_SIZE = 128

D_MIX = D_MODEL
D_CONV = D_MIX // 2
D_ATTN = D_MIX - D_CONV
HEAD_DIM = 64
N_HEADS = D_ATTN // HEAD_DIM
CONV_WIDTH = 31
IDX_HEADS = 8
IDX_DIM = 64
TOPK_MAX = 256
ROPE_THETA = 500000.0
ROT_DIM = HEAD_DIM // 4
IDX_ROT_DIM = IDX_DIM // 4
Q_BLOCK = 128
EPS = 1e-6
IDX_SCALE = (IDX_HEADS * IDX_DIM) ** -0.5
IN_SIZES = (D_CONV, D_CONV, D_CONV, D_ATTN, D_ATTN, D_ATTN, D_ATTN, IDX_HEADS * IDX_DIM, IDX_DIM, IDX_HEADS)
N_IN = sum(IN_SIZES)

kernel_name = "hymba_conformer_dsa_decoder_step"


def _rmsnorm(x, g):
    xf = x.astype(jnp.float32)
    r = xf * lax.rsqrt(jnp.mean(xf * xf, axis=-1, keepdims=True) + EPS)
    return (r * g.astype(jnp.float32)).astype(x.dtype)


def _layernorm(x, g, b):
    xf = x.astype(jnp.float32)
    mu = jnp.mean(xf, axis=-1, keepdims=True)
    var = jnp.mean(jnp.square(xf - mu), axis=-1, keepdims=True)
    r = (xf - mu) * lax.rsqrt(var + EPS)
    return (r * g.astype(jnp.float32) + b.astype(jnp.float32)).astype(x.dtype)


def _rope(x, pos, rot):
    half = rot // 2
    inv = ROPE_THETA ** (-jnp.arange(half, dtype=jnp.float32) * 2.0 / rot)
    ang = pos.astype(jnp.float32)[:, None] * inv[None, :]
    cos = jnp.cos(ang)[None, :, None, :]
    sin = jnp.sin(ang)[None, :, None, :]
    xr = x[..., :rot].astype(jnp.float32)
    x1, x2 = xr[..., :half], xr[..., half:]
    rot_out = jnp.concatenate([x1 * cos - x2 * sin, x1 * sin + x2 * cos], axis=-1).astype(x.dtype)
    return jnp.concatenate([rot_out, x[..., rot:]], axis=-1)


def _project(x, g_pre, w_in, pos):
    b, t = x.shape[0], x.shape[1]
    xn = _rmsnorm(x, g_pre)
    h = jnp.einsum('btd,dn->btn', xn, w_in)
    offsets = [int(o) for o in np.cumsum(IN_SIZES)[:-1]]
    ca, cb, cg, q, k, v, ag, qi, ki, wi = jnp.split(h, offsets, axis=-1)
    q = _rope(q.reshape(b, t, N_HEADS, HEAD_DIM), pos, ROT_DIM)
    k = _rope(k.reshape(b, t, N_HEADS, HEAD_DIM), pos, ROT_DIM)
    v = v.reshape(b, t, N_HEADS, HEAD_DIM)
    qi = _rope(qi.reshape(b, t, IDX_HEADS, IDX_DIM), pos, IDX_ROT_DIM)
    ki = _rope(ki[:, :, None, :], pos, IDX_ROT_DIM)[:, :, 0]
    glu = ca * jax.nn.sigmoid(cb)
    return glu, cg, q, k, v, ag, qi, ki, wi


def _conv_branch(glu_hist, w_dw, b_dw, ln_g, ln_b, gate):
    y = lax.conv_general_dilated(glu_hist, w_dw[:, None, :], (1,), 'VALID',
                                 dimension_numbers=('NWC', 'WIO', 'NWC'),
                                 feature_group_count=D_CONV) + b_dw
    y = _layernorm(y, ln_g, ln_b)
    return jax.nn.silu(y) * jax.nn.silu(gate)


def _indexer_topk(qi, wi, ki, q_pos, topk):
    s = jnp.einsum('bqhd,bsd->bqhs', qi.astype(jnp.float32), ki.astype(jnp.float32))
    score = jnp.einsum('bqhs,bqh->bqs', jax.nn.relu(s), wi.astype(jnp.float32) * IDX_SCALE)
    mask = jnp.arange(ki.shape[1])[None, None, :] <= q_pos[None, :, None]
    score = jnp.where(mask, score, -jnp.inf)
    _, idx = lax.top_k(score, topk)
    valid = idx <= q_pos[None, :, None]
    return idx, valid


def _attend_selected(q, k_sel, v_sel, valid):
    logits = jnp.einsum('bqhd,bqkhd->bqhk', q.astype(jnp.float32), k_sel.astype(jnp.float32)) * (HEAD_DIM ** -0.5)
    logits = jnp.where(valid[:, :, None, :], logits, -jnp.inf)
    p = jax.nn.softmax(logits, axis=-1)
    return jnp.einsum('bqhk,bqkhd->bqhd', p, v_sel.astype(jnp.float32)).astype(q.dtype)


def _finish(x, conv_out, attn_out, attn_gate, w_out, g_post):
    mix = jnp.concatenate([conv_out, attn_out * jax.nn.silu(attn_gate)], axis=-1)
    o = jnp.einsum('btc,cd->btd', mix, w_out)
    return x + _rmsnorm(o, g_post)


_gather_rows = jax.vmap(lambda rows, ii: rows[ii])


def setup_inputs(seed: int = 0) -> dict:
    key = jax.random.key(seed)
    ks = jax.random.split(key, 16)
    n_pages = PAST_LEN // PAGE_SIZE
    n_used = DEC_BATCH * n_pages
    n_pool = n_used + max(1, n_used // 4)
    f32 = jnp.float32
    x_prompt = jax.random.normal(ks[0], (BATCH, SEQ, D_MODEL), f32)
    x_sample = jax.random.normal(ks[1], (DEC_BATCH, DEC_SEQ, D_MODEL), f32)
    cache_k = jax.random.normal(ks[2], (DEPTH, n_pool, PAGE_SIZE, N_HEADS, HEAD_DIM), f32)
    cache_v = jax.random.normal(ks[3], (DEPTH, n_pool, PAGE_SIZE, N_HEADS, HEAD_DIM), f32)
    cache_kidx = jax.random.normal(ks[4], (DEPTH, n_pool, PAGE_SIZE, IDX_DIM), f32)
    state_conv = 0.5 * jax.random.normal(ks[5], (DEPTH, DEC_BATCH, CONV_WIDTH - 1, D_CONV), f32)
    page_table = jax.random.permutation(ks[6], n_pool)[:n_used].reshape(DEC_BATCH, n_pages).astype(jnp.int32)
    g_pre = 1.0 + 0.1 * jax.random.normal(ks[7], (DEPTH, D_MODEL), f32)
    w_in = jax.random.normal(ks[8], (DEPTH, D_MODEL, N_IN), f32) * D_MODEL ** -0.5
    w_dw = jax.random.normal(ks[9], (DEPTH, CONV_WIDTH, D_CONV), f32) * CONV_WIDTH ** -0.5
    b_dw = 0.02 * jax.random.normal(ks[10], (DEPTH, D_CONV), f32)
    ln_g = 1.0 + 0.1 * jax.random.normal(ks[11], (DEPTH, D_CONV), f32)
    ln_b = 0.02 * jax.random.normal(ks[12], (DEPTH, D_CONV), f32)
    w_out = jax.random.normal(ks[13], (DEPTH, D_MIX, D_MODEL), f32) * D_MIX ** -0.5
    g_post = 1.0 + 0.1 * jax.random.normal(ks[14], (DEPTH, D_MODEL), f32)
    return {"x_prompt": x_prompt, "x_sample": x_sample, "cache_k": cache_k, "cache_v": cache_v,
            "cache_kidx": cache_kidx, "state_conv": state_conv, "page_table": page_table,
            "g_pre": g_pre, "w_in": w_in, "w_dw": w_dw, "b_dw": b_dw, "ln_g": ln_g, "ln_b": ln_b,
            "w_out": w_out, "g_post": g_post}


def reference(x_prompt, x_sample, cache_k, cache_v, cache_kidx, state_conv, page_table,
              g_pre, w_in, w_dw, b_dw, ln_g, ln_b, w_out, g_post):
    pos_p = jnp.arange(SEQ)
    pos_s = PAST_LEN + jnp.arange(DEC_SEQ)
    topk_p = min(TOPK_MAX, SEQ // 4)
    topk_s = min(TOPK_MAX, (PAST_LEN + DEC_SEQ) // 4)
    n_blocks = SEQ // Q_BLOCK
    xp, xs = x_prompt, x_sample
    kp_l, vp_l, kip_l, cp_l, ks_l, vs_l, kis_l, cs_l = [], [], [], [], [], [], [], []
    for l in range(DEPTH):
        glu, cg, q, k, v, ag, qi, ki, wi = _project(xp, g_pre[l], w_in[l], pos_p)
        hist = jnp.concatenate([jnp.zeros((xp.shape[0], CONV_WIDTH - 1, D_CONV), glu.dtype), glu], axis=1)
        conv_out = _conv_branch(hist, w_dw[l], b_dw[l], ln_g[l], ln_b[l], cg)

        def block(i, q=q, k=k, v=v, qi=qi, ki=ki, wi=wi):
            start = i * Q_BLOCK
            qb = lax.dynamic_slice_in_dim(q, start, Q_BLOCK, axis=1)
            qib = lax.dynamic_slice_in_dim(qi, start, Q_BLOCK, axis=1)
            wib = lax.dynamic_slice_in_dim(wi, start, Q_BLOCK, axis=1)
            q_pos = start + jnp.arange(Q_BLOCK)
            idx, valid = _indexer_topk(qib, wib, ki, q_pos, topk_p)
            return _attend_selected(qb, _gather_rows(k, idx), _gather_rows(v, idx), valid)

        blocks = lax.map(block, jnp.arange(n_blocks))
        attn = jnp.transpose(blocks, (1, 0, 2, 3, 4)).reshape(xp.shape[0], SEQ, D_ATTN)
        kp_l.append(k)
        vp_l.append(v)
        kip_l.append(ki)
        cp_l.append(hist[:, -(CONV_WIDTH - 1):])
        xp = _finish(xp, conv_out, attn, ag, w_out[l], g_post[l])

        glu_s, cg_s, q_s, k_s, v_s, ag_s, qi_s, ki_s, wi_s = _project(xs, g_pre[l], w_in[l], pos_s)
        hist_s = jnp.concatenate([state_conv[l].astype(glu_s.dtype), glu_s], axis=1)
        conv_out_s = _conv_branch(hist_s, w_dw[l], b_dw[l], ln_g[l], ln_b[l], cg_s)
        ki_past = cache_kidx[l][page_table].reshape(DEC_BATCH, PAST_LEN, IDX_DIM)
        ki_all = jnp.concatenate([ki_past.astype(ki_s.dtype), ki_s], axis=1)
        idx, valid = _indexer_topk(qi_s, wi_s, ki_all, pos_s, topk_s)
        from_cache = (idx < PAST_LEN)[..., None, None]
        pidx = jnp.minimum(idx, PAST_LEN - 1)
        phys = jax.vmap(lambda pt, ii: pt[ii // PAGE_SIZE])(page_table, pidx)
        slot = pidx % PAGE_SIZE
        nidx = jnp.clip(idx - PAST_LEN, 0, DEC_SEQ - 1)
        k_sel = jnp.where(from_cache, cache_k[l][phys, slot].astype(k_s.dtype), _gather_rows(k_s, nidx))
        v_sel = jnp.where(from_cache, cache_v[l][phys, slot].astype(v_s.dtype), _gather_rows(v_s, nidx))
        attn_s = _attend_selected(q_s, k_sel, v_sel, valid).reshape(DEC_BATCH, DEC_SEQ, D_ATTN)
        ks_l.append(k_s)
        vs_l.append(v_s)
        kis_l.append(ki_s)
        cs_l.append(hist_s[:, -(CONV_WIDTH - 1):])
        xs = _finish(xs, conv_out_s, attn_s, ag_s, w_out[l], g_post[l])

    k_prompt = jnp.stack(kp_l)
    v_prompt = jnp.stack(vp_l)
    kidx_prompt = jnp.stack(kip_l)
    conv_prompt = jnp.stack(cp_l)
    k_sample = jnp.stack(ks_l)
    v_sample = jnp.stack(vs_l)
    kidx_sample = jnp.stack(kis_l)
    conv_sample = jnp.stack(cs_l)
    return (xp, xs, k_prompt, v_prompt, kidx_prompt, conv_prompt, k_sample, v_sample, kidx_sample, conv_sample)
```

```python
import functools

import jax
import jax.numpy as jnp
import numpy as np
from jax import lax
from jax.experimental import pallas as pl
from jax.experimental.pallas import tpu as pltpu

F32 = jnp.float32
BF16 = jnp.bfloat16
I32 = jnp.int32

HEAD_DIM = 64
IDX_DIM = 64
IDX_HEADS = 8
CONV_WIDTH = 31
TOPK_MAX = 256
ROPE_THETA = 500000.0
ROT_DIM = HEAD_DIM // 4
EPS = 1e-6
PAGE = 128
LANES = 128
HALO = 32
NEG = -1e30
INT_MIN = -2 ** 31
S_NEGINF = INT_MIN + 0x7FFFFF
S_POSINF = 0x7F800000
LOG2E = 1.4426950408889634
QSCALE = HEAD_DIM ** -0.5 * LOG2E
VMEM_LIMIT = 56 * 1024 * 1024

_NT = (((1,), (1,)), ((), ()))


def _dot(a, b):
    return jnp.dot(a, b, preferred_element_type=F32)


def _dot_nt(a, b):
    return lax.dot_general(a, b, _NT, preferred_element_type=F32)


def _split_bf16(x):
    hi = x.astype(BF16)
    lo = (x - hi.astype(F32)).astype(BF16)
    return hi, lo


def _flip(b):
    return b ^ (lax.shift_right_arithmetic(b, 31) & 0x7FFFFFFF)


def _sortable(x):
    return _flip(lax.bitcast_convert_type(x, I32))


def _unsortable(s):
    return lax.bitcast_convert_type(_flip(s), F32)


def _loop(n, body, init):
    if isinstance(n, int):
        for c in range(n):
            init = body(c, init)
        return init
    return lax.fori_loop(0, n, body, init)


def _silu(x):
    return x * jax.nn.sigmoid(x)


def _tile_lanes(x, n):
    return x if n == 1 else jnp.concatenate([x] * n, axis=1)


def _proj_kernel(x_ref, g_ref, wm_ref, wih_ref, wil_ref, cos_ref, sa_ref, sb_ref,
                 glu_ref, gc_ref, k32_ref, v32_ref, kidx_ref, qm_ref, kb_ref, vb_ref, ga_ref,
                 qic_ref, kic_ref, wi_ref, *, d_conv, d_attn, idx_scale):
    tm = x_ref.shape[0]
    x = x_ref[...]
    ms = jnp.mean(x * x, axis=-1, keepdims=True)
    xn = x * lax.rsqrt(ms + EPS) * g_ref[...]
    xh, xl = _split_bf16(xn)
    cos = cos_ref[...]
    sa = sa_ref[...]
    sb = sb_ref[...]
    lane = lax.broadcasted_iota(I32, (tm, LANES), 1)
    lo_half = lane < HEAD_DIM

    def rope(c):
        return (c * cos + pltpu.roll(c, LANES - ROT_DIM // 2, 1) * sa
                + pltpu.roll(c, ROT_DIM // 2, 1) * sb)

    def mm(a, b):
        return _dot(xh, wm_ref[:, a:b])

    def mm3(a, b):
        wh = wih_ref[:, a:b]
        return _dot(xh, wh) + _dot(xh, wil_ref[:, a:b]) + _dot(xl, wh)

    o = 0
    ca = mm(o, o + d_conv); o += d_conv
    cb = mm(o, o + d_conv); o += d_conv
    glu_ref[...] = ca * jax.nn.sigmoid(cb)
    cg = mm(o, o + d_conv); o += d_conv
    gc_ref[...] = _silu(cg)
    q = mm(o, o + d_attn); o += d_attn
    for c in range(d_attn // LANES):
        qc = rope(q[:, c * LANES:(c + 1) * LANES]) * QSCALE
        qm_ref[:, (2 * c) * LANES:(2 * c + 1) * LANES] = jnp.where(lo_half, qc, 0.0).astype(BF16)
        qm_ref[:, (2 * c + 1) * LANES:(2 * c + 2) * LANES] = jnp.where(lo_half, 0.0, qc).astype(BF16)
    k = mm(o, o + d_attn); o += d_attn
    for c in range(d_attn // LANES):
        kc = rope(k[:, c * LANES:(c + 1) * LANES])
        k32_ref[:, c * LANES:(c + 1) * LANES] = kc
        kb_ref[:, c * LANES:(c + 1) * LANES] = kc.astype(BF16)
    v = mm(o, o + d_attn); o += d_attn
    v32_ref[...] = v
    vb_ref[...] = v.astype(BF16)
    ag = mm(o, o + d_attn); o += d_attn
    ga_ref[...] = _silu(ag)

    nqi = IDX_HEADS * IDX_DIM
    qi = mm3(0, nqi)
    for c in range(nqi // LANES):
        t = rope(qi[:, c * LANES:(c + 1) * LANES])
        r = pltpu.roll(t, HEAD_DIM, 1)
        for hh, dup in ((2 * c, jnp.where(lo_half, t, r)), (2 * c + 1, jnp.where(lo_half, r, t))):
            hi = dup.astype(BF16)
            lo = dup - hi.astype(F32)
            qic_ref[:, hh * 256:hh * 256 + LANES] = hi
            qic_ref[:, hh * 256 + LANES:(hh + 1) * 256] = jnp.where(lo_half, lo, 0.0).astype(BF16)
    kw = mm3(nqi, nqi + 2 * LANES)
    kc = rope(kw[:, :LANES])
    kidx_ref[...] = kc[:, :IDX_DIM]
    kk = kc + pltpu.roll(kc, IDX_DIM, 1)
    hi = kk.astype(BF16).astype(F32)
    lo = kk - hi
    kic_ref[:, :LANES] = jnp.where(lo_half, hi, lo).astype(BF16)
    kic_ref[:, LANES:] = jnp.where(lo_half, hi, 0.0).astype(BF16)
    wi_ref[...] = kw[:, LANES:] * idx_scale


def _rope_tables(pos):
    half = ROT_DIM // 2
    inv = ROPE_THETA ** (-jnp.arange(half, dtype=F32) * 2.0 / ROT_DIM)
    ang = pos.astype(F32)[:, None] * inv[None, :]
    cos, sin = jnp.cos(ang), jnp.sin(ang)
    t = pos.shape[0]
    ones = jnp.ones((t, HEAD_DIM - ROT_DIM), F32)
    z = lambda n: jnp.zeros((t, n), F32)
    c64 = jnp.concatenate([cos, cos, ones], axis=1)
    sa64 = jnp.concatenate([-sin, z(HEAD_DIM - half)], axis=1)
    sb64 = jnp.concatenate([z(half), sin, z(HEAD_DIM - ROT_DIM)], axis=1)
    rep = lambda a: jnp.concatenate([a, a], axis=1)
    return rep(c64), rep(sa64), rep(sb64)


def _project(x, pos, g_pre, w_main, w_idx_hi, w_idx_lo, *, d_conv, d_attn, idx_scale, tm):
    t, d_model = x.shape
    cos, sa, sb = _rope_tables(pos)
    row = lambda w: pl.BlockSpec((tm, w), lambda i: (i, 0))
    full = lambda a: pl.BlockSpec(a.shape, lambda i: (0, 0))
    outs = [
        (d_conv, F32), (d_conv, F32), (d_attn, F32), (d_attn, F32), (IDX_DIM, F32),
        (2 * d_attn, BF16), (d_attn, BF16), (d_attn, BF16), (d_attn, F32),
        (IDX_HEADS * 256, BF16), (256, BF16), (LANES, F32),
    ]
    return pl.pallas_call(
        functools.partial(_proj_kernel, d_conv=d_conv, d_attn=d_attn, idx_scale=idx_scale),
        grid=(t // tm,),
        in_specs=[row(d_model), full(g_pre), full(w_main), full(w_idx_hi), full(w_idx_lo),
                  row(LANES), row(LANES), row(LANES)],
        out_specs=[row(w) for w, _ in outs],
        out_shape=[jax.ShapeDtypeStruct((t, w), dt) for w, dt in outs],
        compiler_params=pltpu.CompilerParams(dimension_semantics=("arbitrary",),
                                             vmem_limit_bytes=VMEM_LIMIT),
        name="project",
    )(x, g_pre, w_main, w_idx_hi, w_idx_lo, cos, sa, sb)


def _select_rows(s_sc, rows, nchunks, cw, topk, all_vis, rsub):
    nrow = rows.stop - rows.start
    nl = cw // LANES
    assert topk <= 2 * LANES and (nl % 2 == 0 or isinstance(nchunks, int)) and nrow % rsub == 0
    inf = float("inf")
    lanef = lax.broadcasted_iota(I32, (rsub, LANES), 1).astype(F32)
    subs = [slice(r0, r0 + rsub) for r0 in range(0, nrow, rsub)]
    cat = lambda parts: parts[0] if len(parts) == 1 else jnp.concatenate(parts, axis=0)

    def blocks(c, sub):
        off = c * cw if isinstance(c, int) else pl.multiple_of(c * cw, cw)
        blk = s_sc[rows.start + sub.start:rows.start + sub.stop, pl.ds(off, cw)]
        return [(blk[:, u * LANES:(u + 1) * LANES], off + u * LANES, (c * nl + u) % 2 if nl % 2 else u % 2)
                for u in range(nl)]

    def key_index(base):
        return lanef + (float(base) if isinstance(base, int) else base.astype(F32))

    def wide(v, sub):
        return jnp.broadcast_to(v[sub], (rsub, LANES))

    def reduce(make_fn, comb, init, lane_red):
        parts = []
        for sub in subs:
            fn = make_fn(sub)

            def body(c, acc, fn=fn, sub=sub):
                for b, base, _ in blocks(c, sub):
                    acc = comb(acc, fn(b, base))
                return acc
            acc = _loop(nchunks, body, jnp.full((rsub, LANES), init, F32))
            parts.append(lane_red(acc, axis=1, keepdims=True))
        return cat(parts)

    def count_ge(v):
        def make(sub):
            vb = wide(v, sub)
            return lambda b, _: jnp.where(b >= vb, 1.0, 0.0)
        return reduce(make, jnp.add, 0.0, jnp.sum)

    los, his = [], []
    for sub in subs:
        def group_max(c, acc, sub=sub):
            acc = list(acc)
            for b, _, par in blocks(c, sub):
                acc[par] = jnp.maximum(acc[par], b)
            return tuple(acc)
        ga, gb = _loop(nchunks, group_max, (jnp.full((rsub, LANES), -inf, F32),) * 2)
        his.append(jnp.max(jnp.maximum(ga, gb), axis=1, keepdims=True))
        los.append(jnp.min(jnp.minimum(ga, gb), axis=1, keepdims=True))
    lo = jnp.maximum(_sortable(cat(los)) - 16, S_NEGINF)
    hi = jnp.minimum(_sortable(cat(his)), S_POSINF - 17) + 17
    done = jnp.where(all_vis, 1.0, 0.0)

    def cond(st):
        return (st[5] > 0) & (st[4] < 40)

    def step(st):
        lo, hi, nlo, done, it, _ = st
        mid = (lax.shift_right_arithmetic(lo, 1) + lax.shift_right_arithmetic(hi, 1) + (lo & hi & 1))
        cnt = count_ge(_unsortable(mid))
        act = done < 0.5
        up = act & (cnt >= topk)
        dn = act & (cnt < topk)
        lo = jnp.where(up, mid, lo)
        nlo = jnp.where(up, cnt, nlo)
        hi = jnp.where(dn, mid, hi)
        done = jnp.where((nlo == topk) | (hi <= lo + 1), 1.0, done)
        return lo, hi, nlo, done, it + 1, (jnp.min(done) < 0.5).astype(I32)

    lo, hi, nlo, done, _, _ = lax.while_loop(
        cond, step, (lo, hi, jnp.full((nrow, 1), -1.0, F32), done, jnp.int32(0),
                     (jnp.min(done) < 0.5).astype(I32)))
    thr = _unsortable(jnp.where(all_vis, S_NEGINF + 1, lo))
    excess = jnp.where(all_vis, 0.0, nlo - topk)
    nmax = jnp.max(excess)

    @pl.when(nmax > 0.0)
    def _():
        def drop(it, _):
            def make_min(sub):
                thrb = wide(thr, sub)
                return lambda b, _: jnp.where(b >= thrb, b, inf)
            vmin = reduce(make_min, jnp.minimum, inf, jnp.min)

            def make_last(sub):
                vminb = wide(vmin, sub)
                return lambda b, base: jnp.where(b == vminb, key_index(base), -1.0)
            last = reduce(make_last, jnp.maximum, -1.0, jnp.max)
            tgt = jnp.where(excess > it.astype(F32), last, -2.0)
            for sub in subs:
                tgtb = wide(tgt, sub)

                def rewrite(c, _, sub=sub, tgtb=tgtb):
                    for b, base, _ in blocks(c, sub):
                        s_sc[rows.start + sub.start:rows.start + sub.stop, pl.ds(base, LANES)] = (
                            jnp.where(key_index(base) == tgtb, -inf, b))
                    return 0
                _loop(nchunks, rewrite, 0)
            return 0
        lax.fori_loop(0, nmax.astype(I32), drop, 0)

    return thr


def _prompt_attn_kernel(qm_ref, qic_ref, wi_ref, kic_ref, kb_ref, vb_ref, o_ref,
                        s_sc, thr_sc, wb_sc, bias_sc, lg_sc, p_sc, alpha_sc, m_sc, l_sc, acc_sc,
                        *, tq, ts, cs, rb, topk, n_heads):
    i = pl.program_id(0)
    j = pl.program_id(1)
    nkt = pl.num_programs(1)
    q0 = i * tq
    nvis = q0 + tq
    nch = (nvis + cs - 1) // cs
    nl = cs // LANES

    @pl.when(j == 0)
    def _():
        m_sc[...] = jnp.full(m_sc.shape, NEG, F32)
        l_sc[...] = jnp.zeros(l_sc.shape, F32)
        acc_sc[...] = jnp.zeros(acc_sc.shape, F32)
        for h in range(IDX_HEADS):
            wb_sc[h] = jnp.broadcast_to(wi_ref[:, h:h + 1], (tq, LANES))
        qpos = q0 + lax.broadcasted_iota(I32, (tq, cs), 0)
        kiota = lax.broadcasted_iota(I32, (tq, cs), 1)

        def score_chunk(c, _):
            off = pl.multiple_of(c * cs, cs)
            kc = kic_ref[pl.ds(off, cs), :]
            acc = jnp.zeros((tq, cs), F32)
            for h in range(IDX_HEADS):
                s = _dot_nt(qic_ref[:, h * 256:(h + 1) * 256], kc)
                acc = acc + jnp.maximum(s, 0.0) * _tile_lanes(wb_sc[h], nl)
            s_sc[:, pl.ds(off, cs)] = jnp.where(kiota + off <= qpos, acc, -float("inf"))
            return 0
        lax.fori_loop(0, nch, score_chunk, 0)

        all_vis = q0 + lax.broadcasted_iota(I32, (tq, 1), 0) < topk
        thr = _select_rows(s_sc, slice(0, tq), nch, cs, topk, all_vis, rsub=min(128, tq))
        thr_sc[...] = jnp.broadcast_to(thr, (tq, LANES))

    @pl.when(j * ts < nvis)
    def _():
        nsub = jnp.minimum((nvis - j * ts + cs - 1) // cs, ts // cs)
        lane = lax.broadcasted_iota(I32, (tq, LANES), 1)
        lo_half = lane < HEAD_DIM

        def sub(u, _):
            loc = pl.multiple_of(u * cs, cs)
            pos = pl.multiple_of(j * ts + loc, cs)
            bias_sc[...] = jnp.where(s_sc[:, pl.ds(pos, cs)] >= _tile_lanes(thr_sc[...], nl), 0.0, NEG)
            for h in range(n_heads):
                pr = h // 2
                sl = h % 2
                kt = kb_ref[pl.ds(loc, cs), pr * LANES:(pr + 1) * LANES]
                lg_sc[sl] = _dot_nt(qm_ref[:, h * LANES:(h + 1) * LANES], kt)
                for r in range(tq // rb):
                    rows = slice(r * rb, (r + 1) * rb)
                    x = lg_sc[sl, rows, :] + bias_sc[rows, :]
                    mx = x[:, :LANES]
                    for u2 in range(1, nl):
                        mx = jnp.maximum(mx, x[:, u2 * LANES:(u2 + 1) * LANES])
                    m_prev = m_sc[h, rows, :]
                    m_new = jnp.maximum(m_prev, jnp.max(mx, axis=1, keepdims=True))
                    p = jnp.exp2(x - _tile_lanes(m_new, nl))
                    ps = p[:, :LANES]
                    for u2 in range(1, nl):
                        ps = ps + p[:, u2 * LANES:(u2 + 1) * LANES]
                    alpha = jnp.exp2(m_prev - m_new)
                    l_sc[h, rows, :] = alpha * l_sc[h, rows, :] + jnp.sum(ps, axis=1, keepdims=True)
                    m_sc[h, rows, :] = m_new
                    alpha_sc[sl, rows, :] = alpha
                    p_sc[sl, rows, :] = p.astype(BF16)
                vt = vb_ref[pl.ds(loc, cs), pr * LANES:(pr + 1) * LANES]
                pv = _dot(p_sc[sl], vt)
                a = acc_sc[pr]
                mine = lo_half if h % 2 == 0 else jnp.logical_not(lo_half)
                acc_sc[pr] = jnp.where(mine, alpha_sc[sl] * a + pv, a)
            return 0
        lax.fori_loop(0, nsub, sub, 0)

    @pl.when(j == nkt - 1)
    def _():
        lane = lax.broadcasted_iota(I32, (tq, LANES), 1)
        lo_half = lane < HEAD_DIM
        for pr in range(n_heads // 2):
            l = jnp.where(lo_half, l_sc[2 * pr], l_sc[2 * pr + 1])
            o_ref[:, pr * LANES:(pr + 1) * LANES] = acc_sc[pr] / l


def _prompt_attn(qm, qic, wi, kic, kb, vb, *, topk, tq, ts, cs):
    t, d_attn = kb.shape
    n_heads = d_attn // HEAD_DIM
    nq, nkt = t // tq, t // ts
    kv_map = lambda i, j: (jnp.minimum(j, ((i + 1) * tq - 1) // ts), 0)
    qrow = lambda w: pl.BlockSpec((tq, w), lambda i, j: (i, 0))
    return pl.pallas_call(
        functools.partial(_prompt_attn_kernel, tq=tq, ts=ts, cs=cs, rb=min(32, tq), topk=topk,
                          n_heads=n_heads),
        grid=(nq, nkt),
        in_specs=[qrow(qm.shape[1]), qrow(qic.shape[1]), qrow(LANES),
                  pl.BlockSpec(kic.shape, lambda i, j: (0, 0)),
                  pl.BlockSpec((ts, d_attn), kv_map), pl.BlockSpec((ts, d_attn), kv_map)],
        out_specs=qrow(d_attn),
        out_shape=jax.ShapeDtypeStruct((t, d_attn), F32),
        scratch_shapes=[
            pltpu.VMEM((tq, t), F32),
            pltpu.VMEM((tq, LANES), F32),
            pltpu.VMEM((IDX_HEADS, tq, LANES), F32),
            pltpu.VMEM((tq, cs), F32),
            pltpu.VMEM((2, tq, cs), F32),
            pltpu.VMEM((2, tq, cs), BF16),
            pltpu.VMEM((2, tq, LANES), F32),
            pltpu.VMEM((n_heads, tq, LANES), F32),
            pltpu.VMEM((n_heads, tq, LANES), F32),
            pltpu.VMEM((n_heads // 2, tq, LANES), F32),
        ],
        compiler_params=pltpu.CompilerParams(dimension_semantics=("arbitrary", "arbitrary"),
                                             vmem_limit_bytes=VMEM_LIMIT),
        name="prompt_attn",
    )(qm, qic, wi, kic, kb, vb)


def _sample_sel_kernel(pt_ref, *refs, ppg, past, topk):
    pages = refs[:ppg]
    qhl_ref, wexp_ref, knh_ref, knl_ref, bias_ref, s_sc = refs[ppg:]
    g = pl.program_id(1)
    ng = pl.num_programs(1)
    nq = bias_ref.shape[0]
    kp = bias_ref.shape[1]
    qhl = qhl_ref[...]
    nr = qhl.shape[0] // 2

    def scores(kh, kl):
        both = _dot_nt(qhl, kh)
        s = both[:nr] + both[nr:] + _dot_nt(qhl[:nr], kl)
        s = jnp.maximum(s, 0.0) * wexp_ref[...]
        acc = s[0:nq]
        for h in range(1, IDX_HEADS):
            acc = acc + s[h * nq:(h + 1) * nq]
        return acc

    for r in range(ppg):
        kh, kl = _split_bf16(pages[r][...])
        off = pl.multiple_of((g * ppg + r) * PAGE, PAGE)
        s_sc[:, pl.ds(off, PAGE)] = scores(kh, kl)

    @pl.when(g == ng - 1)
    def _():
        sn = scores(knh_ref[...], knl_ref[...])
        qi = lax.broadcasted_iota(I32, (nq, PAGE), 0)
        s_sc[:, past:] = jnp.where(lax.broadcasted_iota(I32, (nq, PAGE), 1) <= qi, sn, -float("inf"))
        all_vis = past + lax.broadcasted_iota(I32, (nq, 1), 0) < topk
        thrb = jnp.broadcast_to(
            _select_rows(s_sc, slice(0, nq), kp // PAGE, PAGE, topk, all_vis, rsub=nq), (nq, LANES))
        for c in range(kp // PAGE):
            sl = slice(c * PAGE, (c + 1) * PAGE)
            bias_ref[:, sl] = jnp.where(s_sc[:, sl] >= thrb, 0.0, NEG)


def _sample_sel(page_table, cache_kidx, layer, qhl, wexp, knh, knl, *, topk, ppg):
    nb, npages = page_table.shape
    past = npages * PAGE
    kp = past + PAGE
    nq = qhl.shape[1] // (2 * IDX_HEADS)
    ng = npages // ppg

    def page_spec(r):
        return pl.BlockSpec((None, None, PAGE, IDX_DIM),
                            lambda b, g, pt: (layer, pt[b * npages + g * ppg + r], 0, 0))
    per_seq = lambda a: pl.BlockSpec((None,) + a.shape[1:], lambda b, g, pt: (b, 0, 0))
    return pl.pallas_call(
        functools.partial(_sample_sel_kernel, ppg=ppg, past=past, topk=topk),
        grid_spec=pltpu.PrefetchScalarGridSpec(
            num_scalar_prefetch=1,
            grid=(nb, ng),
            in_specs=[page_spec(r) for r in range(ppg)] + [per_seq(a) for a in (qhl, wexp, knh, knl)],
            out_specs=pl.BlockSpec((None, nq, kp), lambda b, g, pt: (b, 0, 0)),
            scratch_shapes=[pltpu.VMEM((nq, kp), F32)],
        ),
        out_shape=jax.ShapeDtypeStruct((nb, nq, kp), F32),
        compiler_params=pltpu.CompilerParams(dimension_semantics=("arbitrary", "arbitrary"),
                                             vmem_limit_bytes=VMEM_LIMIT),
        name="sample_select",
    )(page_table.reshape(-1), *([cache_kidx] * ppg), qhl, wexp, knh, knl)


def _sample_attn_kernel(pt_ref, *refs, ppg, n_heads):
    kpages = refs[:ppg]
    vpages = refs[ppg:2 * ppg]
    q_ref, bias_ref, biasn_ref, kn_ref, vn_ref, o_ref, m_sc, l_sc, acc_sc = refs[2 * ppg:]
    g = pl.program_id(1)
    ng = pl.num_programs(1)
    nq = q_ref.shape[1]

    def update(k_of, v_of, biases):
        npg = len(biases)
        bias = biases[0] if npg == 1 else jnp.concatenate(biases, axis=1)
        rows = []
        for h in range(n_heads):
            lg = [_dot_nt(q_ref[h], k_of(r, h)) for r in range(npg)]
            rows.append((lg[0] if npg == 1 else jnp.concatenate(lg, axis=1)) + bias)
        lg = jnp.concatenate(rows, axis=0)
        m_prev = m_sc[...]
        m_new = jnp.maximum(m_prev, jnp.max(lg, axis=1, keepdims=True))
        alpha = jnp.exp2(m_prev - m_new)
        p = jnp.exp2(lg - m_new[:, :1])
        l_sc[...] = alpha * l_sc[...] + jnp.sum(p, axis=1, keepdims=True)
        m_sc[...] = m_new
        pvs = []
        for h in range(n_heads):
            ph = p[h * nq:(h + 1) * nq].astype(BF16)
            pv = _dot(ph[:, :PAGE], v_of(0, h))
            for r in range(1, npg):
                pv = pv + _dot(ph[:, r * PAGE:(r + 1) * PAGE], v_of(r, h))
            pvs.append(pv)
        acc_sc[...] = alpha[:, :HEAD_DIM] * acc_sc[...] + jnp.concatenate(pvs, axis=0)

    @pl.when(g == 0)
    def _():
        m_sc[...] = jnp.full(m_sc.shape, NEG, F32)
        l_sc[...] = jnp.zeros(l_sc.shape, F32)
        acc_sc[...] = jnp.zeros(acc_sc.shape, F32)
        update(lambda r, h: kn_ref[h], lambda r, h: vn_ref[h], [biasn_ref[...]])

    def head_rows(page_ref, h):
        flat = page_ref.reshape(PAGE * n_heads, HEAD_DIM)
        return flat[pl.ds(h, PAGE, stride=n_heads), :].astype(BF16)

    update(lambda r, h: head_rows(kpages[r], h), lambda r, h: head_rows(vpages[r], h),
           [bias_ref[:, r * PAGE:(r + 1) * PAGE] for r in range(ppg)])

    @pl.when(g == ng - 1)
    def _():
        o_ref[...] = acc_sc[...] / l_sc[:, :HEAD_DIM]


def _sample_attn(page_table, cache_k, cache_v, layer, q, bias, kn, vn, *, ppg):
    nb, npages = page_table.shape
    n_heads, nq = q.shape[1], q.shape[2]
    ng = npages // ppg

    def page_spec(r):
        return pl.BlockSpec((None, None, PAGE, n_heads, HEAD_DIM),
                            lambda b, g, pt: (layer, pt[b * npages + g * ppg + r], 0, 0, 0))
    per_seq = lambda a: pl.BlockSpec((None,) + a.shape[1:], lambda b, g, pt: (b,) + (0,) * (a.ndim - 1))
    return pl.pallas_call(
        functools.partial(_sample_attn_kernel, ppg=ppg, n_heads=n_heads),
        grid_spec=pltpu.PrefetchScalarGridSpec(
            num_scalar_prefetch=1,
            grid=(nb, ng),
            in_specs=([page_spec(r) for r in range(ppg)] * 2 + [
                per_seq(q),
                pl.BlockSpec((None, nq, ppg * PAGE), lambda b, g, pt: (b, 0, g)),
                pl.BlockSpec((None, nq, PAGE), lambda b, g, pt: (b, 0, npages)),
                per_seq(kn), per_seq(vn)]),
            out_specs=pl.BlockSpec((None, n_heads * nq, HEAD_DIM), lambda b, g, pt: (b, 0, 0)),
            scratch_shapes=[pltpu.VMEM((n_heads * nq, LANES), F32),
                            pltpu.VMEM((n_heads * nq, LANES), F32),
                            pltpu.VMEM((n_heads * nq, HEAD_DIM), F32)],
        ),
        out_shape=jax.ShapeDtypeStruct((nb, n_heads * nq, HEAD_DIM), F32),
        compiler_params=pltpu.CompilerParams(dimension_semantics=("arbitrary", "arbitrary"),
                                             vmem_limit_bytes=VMEM_LIMIT),
        name="sample_attn",
    )(page_table.reshape(-1), *([cache_k] * ppg), *([cache_v] * ppg), q, bias, bias, kn, vn)


def _mix_out(y, gc, attn, ga, x, lng, lnb, wo_ref, gpost, d_conv):
    mu = jnp.mean(y, axis=-1, keepdims=True)
    yc = y - mu
    var = jnp.mean(yc * yc, axis=-1, keepdims=True)
    yn = yc * lax.rsqrt(var + EPS) * lng + lnb
    conv_out = _silu(yn) * gc
    o = (_dot(conv_out.astype(BF16), wo_ref[:d_conv, :])
         + _dot((attn * ga).astype(BF16), wo_ref[d_conv:, :]))
    ms = jnp.mean(o * o, axis=-1, keepdims=True)
    return x + o * lax.rsqrt(ms + EPS) * gpost


def _finish_prompt_kernel(glu_ref, halo_ref, gc_ref, attn_ref, ga_ref, x_ref, wdw_ref, bdw_ref,
                          lng_ref, lnb_ref, wo_ref, gpost_ref, y_ref, win_sc, conv_sc, *, rb):
    i = pl.program_id(0)
    tm, d_conv = glu_ref.shape
    halo = halo_ref[...]
    win_sc[:HALO, :] = jnp.where(i == 0, jnp.zeros_like(halo), halo)
    win_sc[HALO:, :] = glu_ref[...]
    base = HALO - (CONV_WIDTH - 1)
    for r in range(tm // rb):
        acc = jnp.broadcast_to(bdw_ref[...], (rb, d_conv))
        for t in range(CONV_WIDTH):
            s = base + t + r * rb
            acc = acc + wdw_ref[t:t + 1, :] * win_sc[s:s + rb, :]
        conv_sc[r * rb:(r + 1) * rb, :] = acc
    y_ref[...] = _mix_out(conv_sc[...], gc_ref[...], attn_ref[...], ga_ref[...], x_ref[...],
                          lng_ref[...], lnb_ref[...], wo_ref, gpost_ref[...], d_conv)


def _finish_prompt(glu, gc, attn, ga, x, w_dw, b_dw, ln_g, ln_b, w_out, g_post, *, tm):
    t, d_conv = glu.shape
    d_model = x.shape[1]
    row = lambda w: pl.BlockSpec((tm, w), lambda i: (i, 0))
    full = lambda a: pl.BlockSpec(a.shape, lambda i: (0, 0))
    return pl.pallas_call(
        functools.partial(_finish_prompt_kernel, rb=32),
        grid=(t // tm,),
        in_specs=[row(d_conv),
                  pl.BlockSpec((HALO, d_conv), lambda i: (jnp.maximum(i * (tm // HALO) - 1, 0), 0)),
                  row(d_conv), row(attn.shape[1]), row(ga.shape[1]), row(d_model),
                  full(w_dw), full(b_dw), full(ln_g), full(ln_b), full(w_out), full(g_post)],
        out_specs=row(d_model),
        out_shape=jax.ShapeDtypeStruct((t, d_model), F32),
        scratch_shapes=[pltpu.VMEM((tm + HALO, d_conv), F32), pltpu.VMEM((tm, d_conv), F32)],
        compiler_params=pltpu.CompilerParams(dimension_semantics=("arbitrary",),
                                             vmem_limit_bytes=VMEM_LIMIT),
        name="finish_prompt",
    )(glu, glu, gc, attn, ga, x, w_dw, b_dw, ln_g, ln_b, w_out, g_post)


def _finish_sample_kernel(hist_ref, gc_ref, attn_ref, ga_ref, x_ref, wdw_ref, bdw_ref,
                          lng_ref, lnb_ref, wo_ref, gpost_ref, y_ref, *, nq):
    nb, _, d_conv = hist_ref.shape
    acc = jnp.broadcast_to(bdw_ref[...].reshape(1, 1, d_conv), (nb, nq, d_conv))
    for t in range(CONV_WIDTH):
        acc = acc + wdw_ref[t:t + 1, :].reshape(1, 1, d_conv) * hist_ref[:, t:t + nq, :]
    y_ref[...] = _mix_out(acc.reshape(nb * nq, d_conv), gc_ref[...], attn_ref[...], ga_ref[...],
                          x_ref[...], lng_ref[...], lnb_ref[...], wo_ref, gpost_ref[...], d_conv)


def _finish_sample(hist, gc, attn, ga, x, w_dw, b_dw, ln_g, ln_b, w_out, g_post, *, nq):
    args = (hist, gc, attn, ga, x, w_dw, b_dw, ln_g, ln_b, w_out, g_post)
    full = lambda a: pl.BlockSpec(a.shape, lambda i: (0,) * a.ndim)
    return pl.pallas_call(
        functools.partial(_finish_sample_kernel, nq=nq),
        grid=(1,),
        in_specs=[full(a) for a in args],
        out_specs=full(x),
        out_shape=jax.ShapeDtypeStruct(x.shape, F32),
        compiler_params=pltpu.CompilerParams(dimension_semantics=("arbitrary",),
                                             vmem_limit_bytes=VMEM_LIMIT),
        name="finish_sample",
    )(*args)


def _tiles(seq):
    tq = min(256, seq)
    cs = min(512, seq)
    ts = min(2048, seq)
    return tq, ts, cs


def _layer(layer, xp, xs, cache_k, cache_v, cache_kidx, state, page_table, g_pre, w_in, w_dw, b_dw,
           ln_g, ln_b, w_out, g_post, *, past_len):
    seq, d_model = xp.shape
    nb, nq, _ = xs.shape
    d_mix = w_out.shape[0]
    d_conv = d_mix // 2
    d_attn = d_mix - d_conv
    n_heads = d_attn // HEAD_DIM
    nqi = IDX_HEADS * IDX_DIM
    idx_scale = float(nqi) ** -0.5
    n_main = 3 * d_conv + 4 * d_attn

    w_main = w_in[:, :n_main].astype(BF16)
    zcol = lambda n: jnp.zeros((d_model, n), F32)
    w_idx = jnp.concatenate([w_in[:, n_main:n_main + nqi + IDX_DIM], zcol(LANES - IDX_DIM),
                             w_in[:, n_main + nqi + IDX_DIM:], zcol(LANES - IDX_HEADS)], axis=1)
    w_idx_hi = w_idx.astype(BF16)
    w_idx_lo = (w_idx - w_idx_hi.astype(F32)).astype(BF16)
    w_out_b = w_out.astype(BF16)
    w_dw_p = jnp.concatenate([w_dw, jnp.zeros((1, d_conv), F32)], axis=0)
    row = lambda a: a.reshape(1, -1)
    proj = functools.partial(_project, g_pre=row(g_pre), w_main=w_main, w_idx_hi=w_idx_hi,
                             w_idx_lo=w_idx_lo, d_conv=d_conv, d_attn=d_attn, idx_scale=idx_scale)
    fin_w = (w_dw_p, row(b_dw), row(ln_g), row(ln_b), w_out_b, row(g_post))

    tq, ts, cs = _tiles(seq)
    (glu, gc, k32, v32, kidx, qm, kb, vb, ga, qic, kic, wi) = proj(xp, jnp.arange(seq), tm=min(256, seq))
    attn = _prompt_attn(qm, qic, wi, kic, kb, vb, topk=min(TOPK_MAX, seq // 4), tq=tq, ts=ts, cs=cs)
    yp = _finish_prompt(glu, gc, attn, ga, xp, *fin_w, tm=min(256, seq))
    conv_p = glu[seq - (CONV_WIDTH - 1):]

    xs2 = xs.reshape(nb * nq, d_model)
    pos_s = past_len + jnp.tile(jnp.arange(nq), nb)
    (glu_s, gc_s, k32_s, v32_s, kidx_s, qm_s, kb_s, vb_s, ga_s, qic_s, kic_s, wi_s) = proj(
        xs2, pos_s, tm=nb * nq)
    by_head = lambda a, nh: a.reshape(nb, nq, nh, -1).transpose(0, 2, 1, 3)
    qic4 = by_head(qic_s, IDX_HEADS).reshape(nb, IDX_HEADS * nq, 256)
    qhl = jnp.concatenate([qic4[..., :IDX_DIM], qic4[..., LANES:LANES + IDX_DIM]], axis=1)
    wexp = jnp.broadcast_to(by_head(wi_s[:, :IDX_HEADS], IDX_HEADS).reshape(nb, IDX_HEADS * nq, 1),
                            (nb, IDX_HEADS * nq, PAGE))
    pad_keys = lambda a: jnp.pad(a, [(0, 0)] * (a.ndim - 2) + [(0, PAGE - nq), (0, 0)])
    knh = pad_keys(kic_s[:, :IDX_DIM].reshape(nb, nq, IDX_DIM))
    knl = pad_keys(kic_s[:, IDX_DIM:2 * IDX_DIM].reshape(nb, nq, IDX_DIM))
    topk_s = min(TOPK_MAX, (past_len + nq) // 4)
    bias = _sample_sel(page_table, cache_kidx, layer, qhl, wexp, knh, knl, topk=topk_s, ppg=8)
    qpair = qm_s.reshape(nb, nq, n_heads, 2, HEAD_DIM)
    odd = (jnp.arange(n_heads) % 2 == 1)[None, None, :, None]
    q_s = jnp.where(odd, qpair[..., 1, :], qpair[..., 0, :]).transpose(0, 2, 1, 3)
    attn_s = _sample_attn(page_table, cache_k, cache_v, layer, q_s, bias,
                          pad_keys(by_head(kb_s, n_heads)), pad_keys(by_head(vb_s, n_heads)), ppg=8)
    attn_s = attn_s.reshape(nb, n_heads, nq, HEAD_DIM).transpose(0, 2, 1, 3).reshape(nb * nq, d_attn)
    hist_s = jnp.concatenate([state, glu_s.reshape(nb, nq, d_conv),
                              jnp.zeros((nb, 2, d_conv), F32)], axis=1)
    ys = _finish_sample(hist_s, gc_s, attn_s, ga_s, xs2, *fin_w, nq=nq)
    conv_s = hist_s[:, nq:nq + CONV_WIDTH - 1]

    hd = lambda a, lead: a.reshape(lead + (n_heads, HEAD_DIM))
    return (yp, ys.reshape(nb, nq, d_model),
            hd(k32, (1, seq)), hd(v32, (1, seq)), kidx.reshape(1, seq, IDX_DIM), conv_p[None],
            hd(k32_s, (nb, nq)), hd(v32_s, (nb, nq)), kidx_s.reshape(nb, nq, IDX_DIM), conv_s)


def kernel(x_prompt, x_sample, cache_k, cache_v, cache_kidx, state_conv, page_table, g_pre, w_in, w_dw,
           b_dw, ln_g, ln_b, w_out, g_post):
    depth = w_in.shape[0]
    assert x_prompt.shape[0] == 1, "one prompt sequence per call"
    past_len = page_table.shape[1] * PAGE
    xp, xs = x_prompt[0], x_sample
    outs = [[] for _ in range(8)]
    for l in range(depth):
        res = _layer(l, xp, xs, cache_k, cache_v, cache_kidx, state_conv[l], page_table,
                     g_pre[l], w_in[l], w_dw[l], b_dw[l], ln_g[l], ln_b[l], w_out[l], g_post[l],
                     past_len=past_len)
        xp, xs = res[0], res[1]
        for o, r in zip(outs, res[2:]):
            o.append(r)
    return (xp[None], xs) + tuple(jnp.stack(o) for o in outs)
```

```python
import functools

import jax
import jax.numpy as jnp
import numpy as np
from jax import lax
from jax.experimental import pallas as pl
from jax.experimental.pallas import tpu as pltpu

F32 = jnp.float32
BF16 = jnp.bfloat16
I32 = jnp.int32

HEAD_DIM = 64
IDX_DIM = 64
IDX_HEADS = 8
CONV_WIDTH = 31
TOPK_MAX = 256
ROPE_THETA = 500000.0
ROT_DIM = HEAD_DIM // 4
EPS = 1e-6
PAGE = 128
LANES = 128
SUBLANES = 8
HALO = 32
NEG = -1e30
INT_MIN = -2 ** 31
S_NEGINF = INT_MIN + 0x7FFFFF
S_POSINF = 0x7F800000
S_MIN_NORMAL = 0x00800000
MANY_TIES = 3
LOG2E = 1.4426950408889634
QSCALE = HEAD_DIM ** -0.5 * LOG2E
VMEM_LIMIT = 56 * 1024 * 1024

_NT = (((1,), (1,)), ((), ()))


def _dot(a, b):
    return jnp.dot(a, b, preferred_element_type=F32)


def _dot_nt(a, b):
    return lax.dot_general(a, b, _NT, preferred_element_type=F32)


def _split_bf16(x):
    hi = x.astype(BF16)
    lo = (x - hi.astype(F32)).astype(BF16)
    return hi, lo


def _flip(b):
    return b ^ (lax.shift_right_arithmetic(b, 31) & 0x7FFFFFFF)


def _sortable(x):
    return _flip(lax.bitcast_convert_type(x, I32))


def _unsortable(s):
    return lax.bitcast_convert_type(_flip(s), F32)


def _loop(n, body, init):
    if isinstance(n, int):
        for c in range(n):
            init = body(c, init)
        return init
    return lax.fori_loop(0, n, body, init)


def _silu(x):
    return x * jax.nn.sigmoid(x)


def _tile_lanes(x, n):
    return x if n == 1 else jnp.concatenate([x] * n, axis=1)


def _proj_kernel(x_ref, g_ref, wm_ref, wih_ref, wil_ref, cos_ref, sa_ref, sb_ref,
                 glu_ref, gc_ref, k32_ref, v32_ref, kidx_ref, qm_ref, kb_ref, vb_ref, ga_ref,
                 qic_ref, kic_ref, wi_ref, *, d_conv, d_attn, idx_scale):
    tm = x_ref.shape[0]
    x = x_ref[...]
    ms = jnp.mean(x * x, axis=-1, keepdims=True)
    xn = x * lax.rsqrt(ms + EPS) * g_ref[...]
    xh, xl = _split_bf16(xn)
    cos = cos_ref[...]
    sa = sa_ref[...]
    sb = sb_ref[...]
    lane = lax.broadcasted_iota(I32, (tm, LANES), 1)
    lo_half = lane < HEAD_DIM

    def rope(c):
        return (c * cos + pltpu.roll(c, LANES - ROT_DIM // 2, 1) * sa
                + pltpu.roll(c, ROT_DIM // 2, 1) * sb)

    def mm(a, b):
        return _dot(xh, wm_ref[:, a:b])

    def mm3(a, b):
        wh = wih_ref[:, a:b]
        return _dot(xh, wh) + _dot(xh, wil_ref[:, a:b]) + _dot(xl, wh)

    o = 0
    ca = mm(o, o + d_conv); o += d_conv
    cb = mm(o, o + d_conv); o += d_conv
    glu_ref[...] = ca * jax.nn.sigmoid(cb)
    cg = mm(o, o + d_conv); o += d_conv
    gc_ref[...] = _silu(cg)
    q = mm(o, o + d_attn); o += d_attn
    for c in range(d_attn // LANES):
        qc = rope(q[:, c * LANES:(c + 1) * LANES]) * QSCALE
        qm_ref[:, (2 * c) * LANES:(2 * c + 1) * LANES] = jnp.where(lo_half, qc, 0.0).astype(BF16)
        qm_ref[:, (2 * c + 1) * LANES:(2 * c + 2) * LANES] = jnp.where(lo_half, 0.0, qc).astype(BF16)
    k = mm(o, o + d_attn); o += d_attn
    for c in range(d_attn // LANES):
        kc = rope(k[:, c * LANES:(c + 1) * LANES])
        k32_ref[:, c * LANES:(c + 1) * LANES] = kc
        kb_ref[:, c * LANES:(c + 1) * LANES] = kc.astype(BF16)
    v = mm(o, o + d_attn); o += d_attn
    v32_ref[...] = v
    vb_ref[...] = v.astype(BF16)
    ag = mm(o, o + d_attn); o += d_attn
    ga_ref[...] = _silu(ag)

    nqi = IDX_HEADS * IDX_DIM
    qi = mm3(0, nqi)
    for c in range(nqi // LANES):
        t = rope(qi[:, c * LANES:(c + 1) * LANES])
        r = pltpu.roll(t, HEAD_DIM, 1)
        for hh, dup in ((2 * c, jnp.where(lo_half, t, r)), (2 * c + 1, jnp.where(lo_half, r, t))):
            hi = dup.astype(BF16)
            lo = dup - hi.astype(F32)
            qic_ref[:, hh * 256:hh * 256 + LANES] = hi
            qic_ref[:, hh * 256 + LANES:(hh + 1) * 256] = jnp.where(lo_half, lo, 0.0).astype(BF16)
    kw = mm3(nqi, nqi + 2 * LANES)
    kc = rope(kw[:, :LANES])
    kidx_ref[...] = kc[:, :IDX_DIM]
    kk = kc + pltpu.roll(kc, IDX_DIM, 1)
    hi = kk.astype(BF16).astype(F32)
    lo = kk - hi
    kic_ref[:, :LANES] = jnp.where(lo_half, hi, lo).astype(BF16)
    kic_ref[:, LANES:] = jnp.where(lo_half, hi, 0.0).astype(BF16)
    wi_ref[...] = kw[:, LANES:] * idx_scale


def _rope_tables(pos):
    half = ROT_DIM // 2
    inv = ROPE_THETA ** (-jnp.arange(half, dtype=F32) * 2.0 / ROT_DIM)
    ang = pos.astype(F32)[:, None] * inv[None, :]
    cos, sin = jnp.cos(ang), jnp.sin(ang)
    t = pos.shape[0]
    ones = jnp.ones((t, HEAD_DIM - ROT_DIM), F32)
    z = lambda n: jnp.zeros((t, n), F32)
    c64 = jnp.concatenate([cos, cos, ones], axis=1)
    sa64 = jnp.concatenate([-sin, z(HEAD_DIM - half)], axis=1)
    sb64 = jnp.concatenate([z(half), sin, z(HEAD_DIM - ROT_DIM)], axis=1)
    rep = lambda a: jnp.concatenate([a, a], axis=1)
    return rep(c64), rep(sa64), rep(sb64)


def _project(x, pos, g_pre, w_main, w_idx_hi, w_idx_lo, *, d_conv, d_attn, idx_scale, tm):
    t, d_model = x.shape
    cos, sa, sb = _rope_tables(pos)
    row = lambda w: pl.BlockSpec((tm, w), lambda i: (i, 0))
    full = lambda a: pl.BlockSpec(a.shape, lambda i: (0, 0))
    outs = [
        (d_conv, F32), (d_conv, F32), (d_attn, F32), (d_attn, F32), (IDX_DIM, F32),
        (2 * d_attn, BF16), (d_attn, BF16), (d_attn, BF16), (d_attn, F32),
        (IDX_HEADS * 256, BF16), (256, BF16), (LANES, F32),
    ]
    return pl.pallas_call(
        functools.partial(_proj_kernel, d_conv=d_conv, d_attn=d_attn, idx_scale=idx_scale),
        grid=(t // tm,),
        in_specs=[row(d_model), full(g_pre), full(w_main), full(w_idx_hi), full(w_idx_lo),
                  row(LANES), row(LANES), row(LANES)],
        out_specs=[row(w) for w, _ in outs],
        out_shape=[jax.ShapeDtypeStruct((t, w), dt) for w, dt in outs],
        compiler_params=pltpu.CompilerParams(dimension_semantics=("arbitrary",),
                                             vmem_limit_bytes=VMEM_LIMIT),
        name="project",
    )(x, g_pre, w_main, w_idx_hi, w_idx_lo, cos, sa, sb)


def _select_rows(s_sc, rows, nchunks, cw, topk, all_vis, rsub, tie_scratch=None):
    nrow = rows.stop - rows.start
    nl = cw // LANES
    assert topk <= 2 * LANES and (nl % 2 == 0 or isinstance(nchunks, int)) and nrow % rsub == 0
    inf = float("inf")
    subs = [(slice(r0, r0 + rsub), _RowTools(s_sc, rows.start + r0, rsub, nchunks, cw))
            for r0 in range(0, nrow, rsub)]
    cat = lambda parts: parts[0] if len(parts) == 1 else jnp.concatenate(parts, axis=0)

    def count_ge(v):
        return cat([t.reduce(lambda b, _, vb=t.wide(v[sl]): jnp.where(b >= vb, 1.0, 0.0),
                             jnp.add, 0.0, jnp.sum) for sl, t in subs])

    los, his = [], []
    for _, t in subs:
        def group_max(c, acc, t=t):
            acc = list(acc)
            for b, _, par in t.blocks(c):
                acc[par] = jnp.maximum(acc[par], b)
            return tuple(acc)
        ga, gb = _loop(nchunks, group_max, (jnp.full((rsub, LANES), -inf, F32),) * 2)
        his.append(jnp.max(jnp.maximum(ga, gb), axis=1, keepdims=True))
        los.append(jnp.min(jnp.minimum(ga, gb), axis=1, keepdims=True))
    lo = jnp.maximum(_sortable(cat(los)) - 16, S_NEGINF)
    hi = jnp.minimum(_sortable(cat(his)), S_POSINF - 17) + 17
    done = jnp.where(all_vis, 1.0, 0.0)

    def cond(st):
        return (st[5] > 0) & (st[4] < 64)

    def step(st):
        lo, hi, nlo, done, it, _ = st
        mid = (lax.shift_right_arithmetic(lo, 1) + lax.shift_right_arithmetic(hi, 1) + (lo & hi & 1))
        vmid = _sortable(0.5 * _unsortable(lo) + 0.5 * _unsortable(hi))
        vmid = jnp.minimum(jnp.maximum(vmid, lo + 1), hi - 1)
        wide_gap = ((lo ^ hi) < 0) | (hi - lo > (1 << 25))
        mid = jnp.where(wide_gap & (it < 12), vmid, mid)
        mid = jnp.where((lo == 0) & (hi > S_MIN_NORMAL), S_MIN_NORMAL, mid)
        mid = jnp.where((lo < 0) & (hi > 0), 0, mid)
        cnt = count_ge(_unsortable(mid))
        act = done < 0.5
        up = act & (cnt >= topk)
        dn = act & (cnt < topk)
        lo = jnp.where(up, mid, lo)
        nlo = jnp.where(up, cnt, nlo)
        hi = jnp.where(dn, mid, hi)
        done = jnp.where((nlo == topk) | (hi <= lo + 1) | ((lo == 0) & (hi <= S_MIN_NORMAL)), 1.0, done)
        return lo, hi, nlo, done, it + 1, (jnp.min(done) < 0.5).astype(I32)

    lo, hi, nlo, done, _, _ = lax.while_loop(
        cond, step, (lo, hi, jnp.full((nrow, 1), -1.0, F32), done, jnp.int32(0),
                     (jnp.min(done) < 0.5).astype(I32)))
    thr = _unsortable(jnp.where(all_vis, S_NEGINF + 1, lo))
    excess = jnp.where(all_vis, 0.0, nlo - topk)

    @pl.when(jnp.max(excess) > 0.0)
    def _():
        if nrow == SUBLANES:
            _drop_ties(_RowTools(s_sc, rows.start, SUBLANES, nchunks, cw), thr, excess, topk)
            return
        thr_ref, exc_ref = tie_scratch
        thr_ref[...] = jnp.broadcast_to(thr, (nrow, LANES))
        exc_ref[...] = jnp.broadcast_to(excess, (nrow, LANES))

        def group(g, _):
            r0 = pl.multiple_of(g * SUBLANES, SUBLANES)
            exc = exc_ref[pl.ds(r0, SUBLANES), :][:, :1]

            @pl.when(jnp.max(exc) > 0.0)
            def _():
                _drop_ties(_RowTools(s_sc, rows.start + r0, SUBLANES, nchunks, cw),
                           thr_ref[pl.ds(r0, SUBLANES), :][:, :1], exc, topk)
            return 0
        lax.fori_loop(0, nrow // SUBLANES, group, 0)

    return thr


class _RowTools:
    def __init__(self, s_sc, r0, nr, nchunks, cw):
        self.s_sc, self.rows, self.nr, self.nchunks, self.cw = s_sc, pl.ds(r0, nr), nr, nchunks, cw
        self.nl = cw // LANES
        self.lanef = lax.broadcasted_iota(I32, (nr, LANES), 1).astype(F32)

    def blocks(self, c):
        cw, nl = self.cw, self.nl
        off = c * cw if isinstance(c, int) else pl.multiple_of(c * cw, cw)
        blk = self.s_sc[self.rows, pl.ds(off, cw)]
        return [(blk[:, u * LANES:(u + 1) * LANES], off + u * LANES, (c * nl + u) % 2 if nl % 2 else u % 2)
                for u in range(nl)]

    def key_index(self, base):
        return self.lanef + (float(base) if isinstance(base, int) else base.astype(F32))

    def wide(self, v):
        return jnp.broadcast_to(v, (self.nr, LANES))

    def reduce(self, fn, comb, init, lane_red):
        def body(c, acc):
            for b, base, _ in self.blocks(c):
                acc = comb(acc, fn(b, base))
            return acc
        acc = _loop(self.nchunks, body, jnp.full((self.nr, LANES), init, F32))
        return lane_red(acc, axis=1, keepdims=True)

    def erase_where(self, drop_fn):
        def body(c, _):
            for b, base, _ in self.blocks(c):
                self.s_sc[self.rows, pl.ds(base, LANES)] = jnp.where(drop_fn(b, base), -float("inf"), b)
            return 0
        _loop(self.nchunks, body, 0)


def _drop_ties(t, thr, excess, topk):
    inf = float("inf")
    thrb = t.wide(thr)
    many = excess > MANY_TIES
    few = jnp.where(many, 0.0, excess)

    @pl.when(jnp.max(jnp.where(many, 1.0, 0.0)) > 0.0)
    def _():
        need = topk - t.reduce(lambda b, _: jnp.where(b > thrb, 1.0, 0.0), jnp.add, 0.0, jnp.sum)
        nbits = max(1, int(np.ceil(np.log2(t.s_sc.shape[1]))))

        def bis_idx(p, lo_i):
            cand = lo_i + lax.shift_left(jnp.int32(1), nbits - 1 - p)
            cb = t.wide(cand.astype(F32))
            f = t.reduce(lambda b, base: jnp.where((b == thrb) & (t.key_index(base) <= cb), 1.0, 0.0),
                         jnp.add, 0.0, jnp.sum)
            return jnp.where(f < need, cand, lo_i)
        last = (lax.fori_loop(0, nbits, bis_idx, jnp.full(thr.shape, -1, I32)) + 1).astype(F32)
        lb = t.wide(jnp.where(many, last, float(2 ** 30)))
        t.erase_where(lambda b, base: (b == thrb) & (t.key_index(base) > lb))

    nmax = jnp.max(few)

    @pl.when(nmax > 0.0)
    def _():
        def drop(it, _):
            vminb = t.wide(t.reduce(lambda b, _: jnp.where(b >= thrb, b, inf), jnp.minimum, inf, jnp.min))
            last = t.reduce(lambda b, base: jnp.where(b == vminb, t.key_index(base), -1.0),
                            jnp.maximum, -1.0, jnp.max)
            tgtb = t.wide(jnp.where(few > it.astype(F32), last, -2.0))
            t.erase_where(lambda b, base: t.key_index(base) == tgtb)
            return 0
        lax.fori_loop(0, nmax.astype(I32), drop, 0)


def _prompt_attn_kernel(qm_ref, qic_ref, wi_ref, kic_ref, kb_ref, vb_ref, o_ref,
                        s_sc, thr_sc, tie_sc, wb_sc, bias_sc, lg_sc, p_sc, alpha_sc, m_sc, l_sc, acc_sc,
                        *, tq, ts, cs, rb, topk, n_heads):
    i = pl.program_id(0)
    j = pl.program_id(1)
    nkt = pl.num_programs(1)
    q0 = i * tq
    nvis = q0 + tq
    nch = (nvis + cs - 1) // cs
    nl = cs // LANES

    @pl.when(j == 0)
    def _():
        m_sc[...] = jnp.full(m_sc.shape, NEG, F32)
        l_sc[...] = jnp.zeros(l_sc.shape, F32)
        acc_sc[...] = jnp.zeros(acc_sc.shape, F32)
        for h in range(IDX_HEADS):
            wb_sc[h] = jnp.broadcast_to(wi_ref[:, h:h + 1], (tq, LANES))
        qpos = q0 + lax.broadcasted_iota(I32, (tq, cs), 0)
        kiota = lax.broadcasted_iota(I32, (tq, cs), 1)

        def score_chunk(c, _):
            off = pl.multiple_of(c * cs, cs)
            kc = kic_ref[pl.ds(off, cs), :]
            acc = jnp.zeros((tq, cs), F32)
            for h in range(IDX_HEADS):
                s = _dot_nt(qic_ref[:, h * 256:(h + 1) * 256], kc)
                acc = acc + jnp.maximum(s, 0.0) * _tile_lanes(wb_sc[h], nl)
            s_sc[:, pl.ds(off, cs)] = jnp.where(kiota + off <= qpos, acc, -float("inf"))
            return 0
        lax.fori_loop(0, nch, score_chunk, 0)

        all_vis = q0 + lax.broadcasted_iota(I32, (tq, 1), 0) < topk
        thr = _select_rows(s_sc, slice(0, tq), nch, cs, topk, all_vis, rsub=min(128, tq),
                           tie_scratch=(thr_sc, tie_sc))
        thr_sc[...] = jnp.broadcast_to(thr, (tq, LANES))

    @pl.when(j * ts < nvis)
    def _():
        nsub = jnp.minimum((nvis - j * ts + cs - 1) // cs, ts // cs)
        lane = lax.broadcasted_iota(I32, (tq, LANES), 1)
        lo_half = lane < HEAD_DIM

        def sub(u, _):
            loc = pl.multiple_of(u * cs, cs)
            pos = pl.multiple_of(j * ts + loc, cs)
            bias_sc[...] = jnp.where(s_sc[:, pl.ds(pos, cs)] >= _tile_lanes(thr_sc[...], nl), 0.0, NEG)
            for h in range(n_heads):
                pr = h // 2
                sl = h % 2
                kt = kb_ref[pl.ds(loc, cs), pr * LANES:(pr + 1) * LANES]
                lg_sc[sl] = _dot_nt(qm_ref[:, h * LANES:(h + 1) * LANES], kt)
                for r in range(tq // rb):
                    rows = slice(r * rb, (r + 1) * rb)
                    x = lg_sc[sl, rows, :] + bias_sc[rows, :]
                    mx = x[:, :LANES]
                    for u2 in range(1, nl):
                        mx = jnp.maximum(mx, x[:, u2 * LANES:(u2 + 1) * LANES])
                    m_prev = m_sc[h, rows, :]
                    m_new = jnp.maximum(m_prev, jnp.max(mx, axis=1, keepdims=True))
                    p = jnp.exp2(x - _tile_lanes(m_new, nl))
                    ps = p[:, :LANES]
                    for u2 in range(1, nl):
                        ps = ps + p[:, u2 * LANES:(u2 + 1) * LANES]
                    alpha = jnp.exp2(m_prev - m_new)
                    l_sc[h, rows, :] = alpha * l_sc[h, rows, :] + jnp.sum(ps, axis=1, keepdims=True)
                    m_sc[h, rows, :] = m_new
                    alpha_sc[sl, rows, :] = alpha
                    p_sc[sl, rows, :] = p.astype(BF16)
                vt = vb_ref[pl.ds(loc, cs), pr * LANES:(pr + 1) * LANES]
                pv = _dot(p_sc[sl], vt)
                a = acc_sc[pr]
                mine = lo_half if h % 2 == 0 else jnp.logical_not(lo_half)
                acc_sc[pr] = jnp.where(mine, alpha_sc[sl] * a + pv, a)
            return 0
        lax.fori_loop(0, nsub, sub, 0)

    @pl.when(j == nkt - 1)
    def _():
        lane = lax.broadcasted_iota(I32, (tq, LANES), 1)
        lo_half = lane < HEAD_DIM
        for pr in range(n_heads // 2):
            l = jnp.where(lo_half, l_sc[2 * pr], l_sc[2 * pr + 1])
            o_ref[:, pr * LANES:(pr + 1) * LANES] = acc_sc[pr] / l


def _prompt_attn(qm, qic, wi, kic, kb, vb, *, topk, tq, ts, cs):
    t, d_attn = kb.shape
    n_heads = d_attn // HEAD_DIM
    nq, nkt = t // tq, t // ts
    kv_map = lambda i, j: (jnp.minimum(j, ((i + 1) * tq - 1) // ts), 0)
    qrow = lambda w: pl.BlockSpec((tq, w), lambda i, j: (i, 0))
    return pl.pallas_call(
        functools.partial(_prompt_attn_kernel, tq=tq, ts=ts, cs=cs, rb=min(32, tq), topk=topk,
                          n_heads=n_heads),
        grid=(nq, nkt),
        in_specs=[qrow(qm.shape[1]), qrow(qic.shape[1]), qrow(LANES),
                  pl.BlockSpec(kic.shape, lambda i, j: (0, 0)),
                  pl.BlockSpec((ts, d_attn), kv_map), pl.BlockSpec((ts, d_attn), kv_map)],
        out_specs=qrow(d_attn),
        out_shape=jax.ShapeDtypeStruct((t, d_attn), F32),
        scratch_shapes=[
            pltpu.VMEM((tq, t), F32),
            pltpu.VMEM((tq, LANES), F32),
            pltpu.VMEM((tq, LANES), F32),
            pltpu.VMEM((IDX_HEADS, tq, LANES), F32),
            pltpu.VMEM((tq, cs), F32),
            pltpu.VMEM((2, tq, cs), F32),
            pltpu.VMEM((2, tq, cs), BF16),
            pltpu.VMEM((2, tq, LANES), F32),
            pltpu.VMEM((n_heads, tq, LANES), F32),
            pltpu.VMEM((n_heads, tq, LANES), F32),
            pltpu.VMEM((n_heads // 2, tq, LANES), F32),
        ],
        compiler_params=pltpu.CompilerParams(dimension_semantics=("arbitrary", "arbitrary"),
                                             vmem_limit_bytes=VMEM_LIMIT),
        name="prompt_attn",
    )(qm, qic, wi, kic, kb, vb)


def _sample_sel_kernel(pt_ref, *refs, ppg, past, topk):
    pages = refs[:ppg]
    qhl_ref, wexp_ref, knh_ref, knl_ref, bias_ref, s_sc = refs[ppg:]
    g = pl.program_id(1)
    ng = pl.num_programs(1)
    nq = bias_ref.shape[0]
    kp = bias_ref.shape[1]
    qhl = qhl_ref[...]
    nr = qhl.shape[0] // 2

    def scores(kh, kl, mm):
        both = mm(qhl, kh)
        s = both[:nr] + both[nr:] + mm(qhl[:nr], kl)
        s = jnp.maximum(s, 0.0) * wexp_ref[...]
        acc = s[0:nq]
        for h in range(1, IDX_HEADS):
            acc = acc + s[h * nq:(h + 1) * nq]
        return acc

    for r in range(ppg):
        kh, kl = _split_bf16(pages[r][...])
        off = pl.multiple_of((g * ppg + r) * PAGE, PAGE)
        s_sc[:, pl.ds(off, PAGE)] = scores(kh, kl, _dot)

    @pl.when(g == ng - 1)
    def _():
        sn = scores(knh_ref[...], knl_ref[...], _dot_nt)
        qi = lax.broadcasted_iota(I32, (nq, PAGE), 0)
        s_sc[:, past:] = jnp.where(lax.broadcasted_iota(I32, (nq, PAGE), 1) <= qi, sn, -float("inf"))
        all_vis = past + lax.broadcasted_iota(I32, (nq, 1), 0) < topk
        thrb = jnp.broadcast_to(
            _select_rows(s_sc, slice(0, nq), kp // PAGE, PAGE, topk, all_vis, rsub=nq), (nq, LANES))
        for c in range(kp // PAGE):
            sl = slice(c * PAGE, (c + 1) * PAGE)
            bias_ref[:, sl] = jnp.where(s_sc[:, sl] >= thrb, 0.0, NEG)


def _sample_sel(page_table, cache_kidx_t, layer, qhl, wexp, knh, knl, *, topk, ppg):
    nb, npages = page_table.shape
    past = npages * PAGE
    kp = past + PAGE
    nq = qhl.shape[1] // (2 * IDX_HEADS)
    ng = npages // ppg

    def page_spec(r):
        return pl.BlockSpec((None, None, IDX_DIM, PAGE),
                            lambda b, g, pt: (layer, pt[b * npages + g * ppg + r], 0, 0))
    per_seq = lambda a: pl.BlockSpec((None,) + a.shape[1:], lambda b, g, pt: (b, 0, 0))
    return pl.pallas_call(
        functools.partial(_sample_sel_kernel, ppg=ppg, past=past, topk=topk),
        grid_spec=pltpu.PrefetchScalarGridSpec(
            num_scalar_prefetch=1,
            grid=(nb, ng),
            in_specs=[page_spec(r) for r in range(ppg)] + [per_seq(a) for a in (qhl, wexp, knh, knl)],
            out_specs=pl.BlockSpec((None, nq, kp), lambda b, g, pt: (b, 0, 0)),
            scratch_shapes=[pltpu.VMEM((nq, kp), F32)],
        ),
        out_shape=jax.ShapeDtypeStruct((nb, nq, kp), F32),
        compiler_params=pltpu.CompilerParams(dimension_semantics=("arbitrary", "arbitrary"),
                                             vmem_limit_bytes=VMEM_LIMIT),
        name="sample_select",
    )(page_table.reshape(-1), *([cache_kidx_t] * ppg), qhl, wexp, knh, knl)


def _sample_attn_kernel(pt_ref, *refs, ppg, n_heads):
    kpages = refs[:ppg]
    vpages = refs[ppg:2 * ppg]
    qbd_ref, bias_ref, biasn_ref, kn_ref, vn_ref, o_ref, m_sc, l_sc, acc_sc = refs[2 * ppg:]
    g = pl.program_id(1)
    ng = pl.num_programs(1)
    nq = o_ref.shape[0]
    d_attn = o_ref.shape[1]
    qbd = qbd_ref[...]

    def heads(b):
        return jnp.concatenate([b] * n_heads, axis=0)

    def update(lg, pv_of):
        m_prev = m_sc[...]
        m_new = jnp.maximum(m_prev, jnp.max(lg, axis=1, keepdims=True))
        alpha = jnp.exp2(m_prev - m_new)
        p = jnp.exp2(lg - m_new[:, :1])
        l_sc[...] = alpha * l_sc[...] + jnp.sum(p, axis=1, keepdims=True)
        m_sc[...] = m_new
        acc_sc[...] = alpha[:, :1] * acc_sc[...] + pv_of(p.astype(BF16))

    @pl.when(g == 0)
    def _():
        m_sc[...] = jnp.full(m_sc.shape, NEG, F32)
        l_sc[...] = jnp.zeros(l_sc.shape, F32)
        acc_sc[...] = jnp.zeros(acc_sc.shape, F32)
        update(_dot_nt(qbd, kn_ref[...]) + heads(biasn_ref[...]), lambda p: _dot(p, vn_ref[...]))

    lg = jnp.concatenate(
        [_dot(qbd, kpages[r][...].reshape(d_attn, PAGE).astype(BF16)) for r in range(ppg)], axis=1)

    def pv_pages(p):
        pv = _dot_nt(p[:, :PAGE], vpages[0][...].reshape(d_attn, PAGE).astype(BF16))
        for r in range(1, ppg):
            pv = pv + _dot_nt(p[:, r * PAGE:(r + 1) * PAGE],
                              vpages[r][...].reshape(d_attn, PAGE).astype(BF16))
        return pv
    update(lg + heads(bias_ref[...]), pv_pages)

    @pl.when(g == ng - 1)
    def _():
        res = acc_sc[...] / l_sc[:, :1]
        lane = lax.broadcasted_iota(I32, (nq, d_attn), 1)
        out = jnp.zeros((nq, d_attn), F32)
        for h in range(n_heads):
            mine = (lane >= h * HEAD_DIM) & (lane < (h + 1) * HEAD_DIM)
            out = jnp.where(mine, res[h * nq:(h + 1) * nq], out)
        o_ref[...] = out


def _sample_attn(page_table, cache_k_t, cache_v_t, layer, qbd, bias, kn, vn, *, ppg):
    nb, npages = page_table.shape
    n_heads = cache_k_t.shape[2]
    d_attn = n_heads * HEAD_DIM
    nq = bias.shape[1]
    ng = npages // ppg

    def page_spec(r):
        return pl.BlockSpec((None, None, n_heads, HEAD_DIM, PAGE),
                            lambda b, g, pt: (layer, pt[b * npages + g * ppg + r], 0, 0, 0))
    per_seq = lambda a: pl.BlockSpec((None,) + a.shape[1:], lambda b, g, pt: (b, 0, 0))
    return pl.pallas_call(
        functools.partial(_sample_attn_kernel, ppg=ppg, n_heads=n_heads),
        grid_spec=pltpu.PrefetchScalarGridSpec(
            num_scalar_prefetch=1,
            grid=(nb, ng),
            in_specs=([page_spec(r) for r in range(ppg)] * 2 + [
                per_seq(qbd),
                pl.BlockSpec((None, nq, ppg * PAGE), lambda b, g, pt: (b, 0, g)),
                pl.BlockSpec((None, nq, PAGE), lambda b, g, pt: (b, 0, npages)),
                per_seq(kn), per_seq(vn)]),
            out_specs=pl.BlockSpec((None, nq, d_attn), lambda b, g, pt: (b, 0, 0)),
            scratch_shapes=[pltpu.VMEM((n_heads * nq, LANES), F32),
                            pltpu.VMEM((n_heads * nq, LANES), F32),
                            pltpu.VMEM((n_heads * nq, d_attn), F32)],
        ),
        out_shape=jax.ShapeDtypeStruct((nb, nq, d_attn), F32),
        compiler_params=pltpu.CompilerParams(dimension_semantics=("arbitrary", "arbitrary"),
                                             vmem_limit_bytes=VMEM_LIMIT),
        name="sample_attn",
    )(page_table.reshape(-1), *([cache_k_t] * ppg), *([cache_v_t] * ppg), qbd, bias, bias, kn, vn)


def _mix_out(y, gc, attn, ga, x, lng, lnb, wo_ref, gpost, d_conv):
    mu = jnp.mean(y, axis=-1, keepdims=True)
    yc = y - mu
    var = jnp.mean(yc * yc, axis=-1, keepdims=True)
    yn = yc * lax.rsqrt(var + EPS) * lng + lnb
    conv_out = _silu(yn) * gc
    o = (_dot(conv_out.astype(BF16), wo_ref[:d_conv, :])
         + _dot((attn * ga).astype(BF16), wo_ref[d_conv:, :]))
    ms = jnp.mean(o * o, axis=-1, keepdims=True)
    return x + o * lax.rsqrt(ms + EPS) * gpost


def _finish_prompt_kernel(glu_ref, halo_ref, gc_ref, attn_ref, ga_ref, x_ref, wdw_ref, bdw_ref,
                          lng_ref, lnb_ref, wo_ref, gpost_ref, y_ref, win_sc, conv_sc, *, rb):
    i = pl.program_id(0)
    tm, d_conv = glu_ref.shape
    halo = halo_ref[...]
    win_sc[:HALO, :] = jnp.where(i == 0, jnp.zeros_like(halo), halo)
    win_sc[HALO:, :] = glu_ref[...]
    base = HALO - (CONV_WIDTH - 1)
    for r in range(tm // rb):
        acc = jnp.broadcast_to(bdw_ref[...], (rb, d_conv))
        for t in range(CONV_WIDTH):
            s = base + t + r * rb
            acc = acc + wdw_ref[t:t + 1, :] * win_sc[s:s + rb, :]
        conv_sc[r * rb:(r + 1) * rb, :] = acc
    y_ref[...] = _mix_out(conv_sc[...], gc_ref[...], attn_ref[...], ga_ref[...], x_ref[...],
                          lng_ref[...], lnb_ref[...], wo_ref, gpost_ref[...], d_conv)


def _finish_prompt(glu, gc, attn, ga, x, w_dw, b_dw, ln_g, ln_b, w_out, g_post, *, tm):
    t, d_conv = glu.shape
    d_model = x.shape[1]
    row = lambda w: pl.BlockSpec((tm, w), lambda i: (i, 0))
    full = lambda a: pl.BlockSpec(a.shape, lambda i: (0, 0))
    return pl.pallas_call(
        functools.partial(_finish_prompt_kernel, rb=32),
        grid=(t // tm,),
        in_specs=[row(d_conv),
                  pl.BlockSpec((HALO, d_conv), lambda i: (jnp.maximum(i * (tm // HALO) - 1, 0), 0)),
                  row(d_conv), row(attn.shape[1]), row(ga.shape[1]), row(d_model),
                  full(w_dw), full(b_dw), full(ln_g), full(ln_b), full(w_out), full(g_post)],
        out_specs=row(d_model),
        out_shape=jax.ShapeDtypeStruct((t, d_model), F32),
        scratch_shapes=[pltpu.VMEM((tm + HALO, d_conv), F32), pltpu.VMEM((tm, d_conv), F32)],
        compiler_params=pltpu.CompilerParams(dimension_semantics=("arbitrary",),
                                             vmem_limit_bytes=VMEM_LIMIT),
        name="finish_prompt",
    )(glu, glu, gc, attn, ga, x, w_dw, b_dw, ln_g, ln_b, w_out, g_post)


def _finish_sample_kernel(hist_ref, gc_ref, attn_ref, ga_ref, x_ref, wdw_ref, bdw_ref,
                          lng_ref, lnb_ref, wo_ref, gpost_ref, y_ref, *, nq):
    nb, _, d_conv = hist_ref.shape
    acc = jnp.broadcast_to(bdw_ref[...].reshape(1, 1, d_conv), (nb, nq, d_conv))
    for t in range(CONV_WIDTH):
        acc = acc + wdw_ref[t:t + 1, :].reshape(1, 1, d_conv) * hist_ref[:, t:t + nq, :]
    y_ref[...] = _mix_out(acc.reshape(nb * nq, d_conv), gc_ref[...], attn_ref[...], ga_ref[...],
                          x_ref[...], lng_ref[...], lnb_ref[...], wo_ref, gpost_ref[...], d_conv)


def _finish_sample(hist, gc, attn, ga, x, w_dw, b_dw, ln_g, ln_b, w_out, g_post, *, nq):
    args = (hist, gc, attn, ga, x, w_dw, b_dw, ln_g, ln_b, w_out, g_post)
    full = lambda a: pl.BlockSpec(a.shape, lambda i: (0,) * a.ndim)
    return pl.pallas_call(
        functools.partial(_finish_sample_kernel, nq=nq),
        grid=(1,),
        in_specs=[full(a) for a in args],
        out_specs=full(x),
        out_shape=jax.ShapeDtypeStruct(x.shape, F32),
        compiler_params=pltpu.CompilerParams(dimension_semantics=("arbitrary",),
                                             vmem_limit_bytes=VMEM_LIMIT),
        name="finish_sample",
    )(*args)


def _tiles(seq):
    tq = min(256, seq)
    cs = min(512, seq)
    ts = min(2048, seq)
    return tq, ts, cs


def _layer(layer, xp, xs, cache_k, cache_v, cache_kidx, state, page_table, g_pre, w_in, w_dw, b_dw,
           ln_g, ln_b, w_out, g_post, *, past_len):
    seq, d_model = xp.shape
    nb, nq, _ = xs.shape
    d_mix = w_out.shape[0]
    d_conv = d_mix // 2
    d_attn = d_mix - d_conv
    n_heads = d_attn // HEAD_DIM
    nqi = IDX_HEADS * IDX_DIM
    idx_scale = float(nqi) ** -0.5
    n_main = 3 * d_conv + 4 * d_attn

    w_main = w_in[:, :n_main].astype(BF16)
    zcol = lambda n: jnp.zeros((d_model, n), F32)
    w_idx = jnp.concatenate([w_in[:, n_main:n_main + nqi + IDX_DIM], zcol(LANES - IDX_DIM),
                             w_in[:, n_main + nqi + IDX_DIM:], zcol(LANES - IDX_HEADS)], axis=1)
    w_idx_hi = w_idx.astype(BF16)
    w_idx_lo = (w_idx - w_idx_hi.astype(F32)).astype(BF16)
    w_out_b = w_out.astype(BF16)
    w_dw_p = jnp.concatenate([w_dw, jnp.zeros((1, d_conv), F32)], axis=0)
    row = lambda a: a.reshape(1, -1)
    proj = functools.partial(_project, g_pre=row(g_pre), w_main=w_main, w_idx_hi=w_idx_hi,
                             w_idx_lo=w_idx_lo, d_conv=d_conv, d_attn=d_attn, idx_scale=idx_scale)
    fin_w = (w_dw_p, row(b_dw), row(ln_g), row(ln_b), w_out_b, row(g_post))

    tq, ts, cs = _tiles(seq)
    (glu, gc, k32, v32, kidx, qm, kb, vb, ga, qic, kic, wi) = proj(xp, jnp.arange(seq), tm=min(256, seq))
    attn = _prompt_attn(qm, qic, wi, kic, kb, vb, topk=min(TOPK_MAX, seq // 4), tq=tq, ts=ts, cs=cs)
    yp = _finish_prompt(glu, gc, attn, ga, xp, *fin_w, tm=min(256, seq))
    conv_p = glu[seq - (CONV_WIDTH - 1):]

    xs2 = xs.reshape(nb * nq, d_model)
    pos_s = past_len + jnp.tile(jnp.arange(nq), nb)
    (glu_s, gc_s, k32_s, v32_s, kidx_s, qm_s, kb_s, vb_s, ga_s, qic_s, kic_s, wi_s) = proj(
        xs2, pos_s, tm=nb * nq)
    by_head = lambda a, nh: a.reshape(nb, nq, nh, -1).transpose(0, 2, 1, 3)
    qic4 = by_head(qic_s, IDX_HEADS).reshape(nb, IDX_HEADS * nq, 256)
    qhl = jnp.concatenate([qic4[..., :IDX_DIM], qic4[..., LANES:LANES + IDX_DIM]], axis=1)
    wexp = jnp.broadcast_to(by_head(wi_s[:, :IDX_HEADS], IDX_HEADS).reshape(nb, IDX_HEADS * nq, 1),
                            (nb, IDX_HEADS * nq, PAGE))
    pad_keys = lambda a: jnp.pad(a, [(0, 0)] * (a.ndim - 2) + [(0, PAGE - nq), (0, 0)])
    knh = pad_keys(kic_s[:, :IDX_DIM].reshape(nb, nq, IDX_DIM))
    knl = pad_keys(kic_s[:, IDX_DIM:2 * IDX_DIM].reshape(nb, nq, IDX_DIM))
    topk_s = min(TOPK_MAX, (past_len + nq) // 4)
    bias = _sample_sel(page_table, cache_kidx.transpose(0, 1, 3, 2), layer, qhl, wexp, knh, knl,
                       topk=topk_s, ppg=8)
    onehot = (jnp.arange(n_heads)[:, None] // 2 == jnp.arange(n_heads // 2)[None, :]).astype(BF16)
    qbd = (by_head(qm_s, n_heads)[:, :, :, None, :]
           * onehot[None, :, None, :, None]).reshape(nb, n_heads * nq, d_attn)
    attn_s = _sample_attn(page_table, cache_k.transpose(0, 1, 3, 4, 2), cache_v.transpose(0, 1, 3, 4, 2),
                          layer, qbd, bias, pad_keys(kb_s.reshape(nb, nq, d_attn)),
                          pad_keys(vb_s.reshape(nb, nq, d_attn)), ppg=8).reshape(nb * nq, d_attn)
    hist_s = jnp.concatenate([state, glu_s.reshape(nb, nq, d_conv),
                              jnp.zeros((nb, 2, d_conv), F32)], axis=1)
    ys = _finish_sample(hist_s, gc_s, attn_s, ga_s, xs2, *fin_w, nq=nq)
    conv_s = hist_s[:, nq:nq + CONV_WIDTH - 1]

    hd = lambda a, lead: a.reshape(lead + (n_heads, HEAD_DIM))
    return (yp, ys.reshape(nb, nq, d_model),
            hd(k32, (1, seq)), hd(v32, (1, seq)), kidx.reshape(1, seq, IDX_DIM), conv_p[None],
            hd(k32_s, (nb, nq)), hd(v32_s, (nb, nq)), kidx_s.reshape(nb, nq, IDX_DIM), conv_s)


def kernel(x_prompt, x_sample, cache_k, cache_v, cache_kidx, state_conv, page_table, g_pre, w_in, w_dw,
           b_dw, ln_g, ln_b, w_out, g_post):
    depth = w_in.shape[0]
    assert x_prompt.shape[0] == 1, "one prompt sequence per call"
    past_len = page_table.shape[1] * PAGE
    xp, xs = x_prompt[0], x_sample
    outs = [[] for _ in range(8)]
    for l in range(depth):
        res = _layer(l, xp, xs, cache_k, cache_v, cache_kidx, state_conv[l], page_table,
                     g_pre[l], w_in[l], w_dw[l], b_dw[l], ln_g[l], ln_b[l], w_out[l], g_post[l],
                     past_len=past_len)
        xp, xs = res[0], res[1]
        for o, r in zip(outs, res[2:]):
            o.append(r)
    return (xp[None], xs) + tuple(jnp.stack(o) for o in outs)
```

```python
import functools

import jax
import jax.numpy as jnp
import numpy as np
from jax import lax
from jax.experimental import pallas as pl
from jax.experimental.pallas import tpu as pltpu

F32 = jnp.float32
BF16 = jnp.bfloat16
I32 = jnp.int32

HEAD_DIM = 64
IDX_DIM = 64
IDX_HEADS = 8
CONV_WIDTH = 31
TOPK_MAX = 256
ROPE_THETA = 500000.0
ROT_DIM = HEAD_DIM // 4
EPS = 1e-6
PAGE = 128
LANES = 128
SUBLANES = 8
HALO = 32
NEG = -2.0 ** 100
INT_MIN = -2 ** 31
S_NEGINF = INT_MIN + 0x7FFFFF
S_POSINF = 0x7F800000
S_MIN_NORMAL = 0x00800000
MANY_TIES = 3
LOG2E = 1.4426950408889634
QSCALE = HEAD_DIM ** -0.5 * LOG2E
VMEM_LIMIT = 56 * 1024 * 1024

_NT = (((1,), (1,)), ((), ()))


def _dot(a, b):
    return jnp.dot(a, b, preferred_element_type=F32)


def _dot_nt(a, b):
    return lax.dot_general(a, b, _NT, preferred_element_type=F32)


def _split_bf16(x):
    hi = x.astype(BF16)
    lo = (x - hi.astype(F32)).astype(BF16)
    return hi, lo


def _flip(b):
    return b ^ (lax.shift_right_arithmetic(b, 31) & 0x7FFFFFFF)


def _sortable(x):
    return _flip(lax.bitcast_convert_type(x, I32))


def _unsortable(s):
    return lax.bitcast_convert_type(_flip(s), F32)


def _loop(n, body, init):
    if isinstance(n, int):
        for c in range(n):
            init = body(c, init)
        return init
    return lax.fori_loop(0, n, body, init)


def _silu(x):
    return x * jax.nn.sigmoid(x)


def _tile_lanes(x, n):
    return x if n == 1 else jnp.concatenate([x] * n, axis=1)


def _proj_kernel(x_ref, g_ref, wm_ref, wih_ref, wil_ref, cos_ref, sa_ref, sb_ref,
                 glu_ref, gc_ref, k32_ref, v32_ref, kidx_ref, qm_ref, kb_ref, vb_ref, ga_ref,
                 qic_ref, kic_ref, wi_ref, kt_ref, *, d_conv, d_attn, idx_scale):
    tm = x_ref.shape[0]
    x = x_ref[...]
    ms = jnp.mean(x * x, axis=-1, keepdims=True)
    xn = x * lax.rsqrt(ms + EPS) * g_ref[...]
    xh, xl = _split_bf16(xn)
    cos = cos_ref[...]
    sa = sa_ref[...]
    sb = sb_ref[...]
    lane = lax.broadcasted_iota(I32, (tm, LANES), 1)
    lo_half = lane < HEAD_DIM

    def rope(c):
        return (c * cos + pltpu.roll(c, LANES - ROT_DIM // 2, 1) * sa
                + pltpu.roll(c, ROT_DIM // 2, 1) * sb)

    def mm(a, b):
        return _dot(xh, wm_ref[:, a:b])

    def mm3(a, b):
        wh = wih_ref[:, a:b]
        return _dot(xh, wh) + _dot(xh, wil_ref[:, a:b]) + _dot(xl, wh)

    o = 0
    ca = mm(o, o + d_conv); o += d_conv
    cb = mm(o, o + d_conv); o += d_conv
    glu_ref[...] = ca * jax.nn.sigmoid(cb)
    cg = mm(o, o + d_conv); o += d_conv
    gc_ref[...] = _silu(cg)
    q = mm(o, o + d_attn); o += d_attn
    for c in range(d_attn // LANES):
        qc = rope(q[:, c * LANES:(c + 1) * LANES]) * QSCALE
        qm_ref[:, (2 * c) * LANES:(2 * c + 1) * LANES] = jnp.where(lo_half, qc, 0.0).astype(BF16)
        qm_ref[:, (2 * c + 1) * LANES:(2 * c + 2) * LANES] = jnp.where(lo_half, 0.0, qc).astype(BF16)
    k = mm(o, o + d_attn); o += d_attn
    for c in range(d_attn // LANES):
        kc = rope(k[:, c * LANES:(c + 1) * LANES])
        k32_ref[:, c * LANES:(c + 1) * LANES] = kc
        kb_ref[:, c * LANES:(c + 1) * LANES] = kc.astype(BF16)
        kt_ref[c * LANES:(c + 1) * LANES, :] = kc.T.astype(BF16)
    v = mm(o, o + d_attn); o += d_attn
    v32_ref[...] = v
    vb_ref[...] = v.astype(BF16)
    ag = mm(o, o + d_attn); o += d_attn
    ga_ref[...] = _silu(ag)

    nqi = IDX_HEADS * IDX_DIM
    qi = mm3(0, nqi)
    for c in range(nqi // LANES):
        t = rope(qi[:, c * LANES:(c + 1) * LANES])
        r = pltpu.roll(t, HEAD_DIM, 1)
        for hh, dup in ((2 * c, jnp.where(lo_half, t, r)), (2 * c + 1, jnp.where(lo_half, r, t))):
            hi = dup.astype(BF16)
            lo = dup - hi.astype(F32)
            qic_ref[:, hh * 256:hh * 256 + LANES] = hi
            qic_ref[:, hh * 256 + LANES:(hh + 1) * 256] = jnp.where(lo_half, lo, 0.0).astype(BF16)
    kw = mm3(nqi, nqi + 2 * LANES)
    kc = rope(kw[:, :LANES])
    kidx_ref[...] = kc[:, :IDX_DIM]
    kk = kc + pltpu.roll(kc, IDX_DIM, 1)
    hi = kk.astype(BF16).astype(F32)
    lo = kk - hi
    kic_ref[:, :LANES] = jnp.where(lo_half, hi, lo).astype(BF16)
    kic_ref[:, LANES:] = jnp.where(lo_half, hi, 0.0).astype(BF16)
    wi_ref[...] = kw[:, LANES:] * idx_scale


def _rope_tables(pos):
    half = ROT_DIM // 2
    inv = ROPE_THETA ** (-jnp.arange(half, dtype=F32) * 2.0 / ROT_DIM)
    ang = pos.astype(F32)[:, None] * inv[None, :]
    cos, sin = jnp.cos(ang), jnp.sin(ang)
    t = pos.shape[0]
    ones = jnp.ones((t, HEAD_DIM - ROT_DIM), F32)
    z = lambda n: jnp.zeros((t, n), F32)
    c64 = jnp.concatenate([cos, cos, ones], axis=1)
    sa64 = jnp.concatenate([-sin, z(HEAD_DIM - half)], axis=1)
    sb64 = jnp.concatenate([z(half), sin, z(HEAD_DIM - ROT_DIM)], axis=1)
    rep = lambda a: jnp.concatenate([a, a], axis=1)
    return rep(c64), rep(sa64), rep(sb64)


def _project(x, pos, g_pre, w_main, w_idx_hi, w_idx_lo, *, d_conv, d_attn, idx_scale, tm):
    t, d_model = x.shape
    cos, sa, sb = _rope_tables(pos)
    row = lambda w: pl.BlockSpec((tm, w), lambda i: (i, 0))
    full = lambda a: pl.BlockSpec(a.shape, lambda i: (0, 0))
    outs = [
        (d_conv, F32), (d_conv, F32), (d_attn, F32), (d_attn, F32), (IDX_DIM, F32),
        (2 * d_attn, BF16), (d_attn, BF16), (d_attn, BF16), (d_attn, F32),
        (IDX_HEADS * 256, BF16), (256, BF16), (LANES, F32),
    ]
    return pl.pallas_call(
        functools.partial(_proj_kernel, d_conv=d_conv, d_attn=d_attn, idx_scale=idx_scale),
        grid=(t // tm,),
        in_specs=[row(d_model), full(g_pre), full(w_main), full(w_idx_hi), full(w_idx_lo),
                  row(LANES), row(LANES), row(LANES)],
        out_specs=[row(w) for w, _ in outs] + [pl.BlockSpec((d_attn, tm), lambda i: (0, i))],
        out_shape=([jax.ShapeDtypeStruct((t, w), dt) for w, dt in outs]
                   + [jax.ShapeDtypeStruct((d_attn, t), BF16)]),
        compiler_params=pltpu.CompilerParams(dimension_semantics=("arbitrary",),
                                             vmem_limit_bytes=VMEM_LIMIT),
        name="project",
    )(x, g_pre, w_main, w_idx_hi, w_idx_lo, cos, sa, sb)


def _select_rows(s_sc, rows, nchunks, cw, topk, all_vis, rsub, tie_scratch=None):
    nrow = rows.stop - rows.start
    nl = cw // LANES
    assert topk <= 2 * LANES and (nl % 2 == 0 or isinstance(nchunks, int)) and nrow % rsub == 0
    inf = float("inf")
    subs = [(slice(r0, r0 + rsub), _RowTools(s_sc, rows.start + r0, rsub, nchunks, cw))
            for r0 in range(0, nrow, rsub)]
    cat = lambda parts: parts[0] if len(parts) == 1 else jnp.concatenate(parts, axis=0)

    def count_ge(v):
        return cat([t.reduce(lambda b, _, vb=t.wide(v[sl]): jnp.where(b >= vb, 1.0, 0.0),
                             jnp.add, 0.0, jnp.sum) for sl, t in subs])

    los, his = [], []
    for _, t in subs:
        def group_max(c, acc, t=t):
            acc = list(acc)
            for b, _, par in t.blocks(c):
                acc[par] = jnp.maximum(acc[par], b)
            return tuple(acc)
        ga, gb = _loop(nchunks, group_max, (jnp.full((rsub, LANES), -inf, F32),) * 2)
        his.append(jnp.max(jnp.maximum(ga, gb), axis=1, keepdims=True))
        los.append(jnp.min(jnp.minimum(ga, gb), axis=1, keepdims=True))
    lo = jnp.maximum(_sortable(cat(los)) - 16, S_NEGINF)
    hi = jnp.minimum(_sortable(cat(his)), S_POSINF - 17) + 17
    done = jnp.where(all_vis, 1.0, 0.0)

    def cond(st):
        return (st[5] > 0) & (st[4] < 64)

    def step(st):
        lo, hi, nlo, done, it, _ = st
        mid = (lax.shift_right_arithmetic(lo, 1) + lax.shift_right_arithmetic(hi, 1) + (lo & hi & 1))
        vmid = _sortable(0.5 * _unsortable(lo) + 0.5 * _unsortable(hi))
        vmid = jnp.minimum(jnp.maximum(vmid, lo + 1), hi - 1)
        far = jnp.where((lo ^ hi) < 0, 1, jnp.where(hi - lo > (1 << 25), 1, 0))
        mid = jnp.where(far * (it < 12).astype(I32) > 0, vmid, mid)
        mid = jnp.where(lo == 0, jnp.where(hi > S_MIN_NORMAL, S_MIN_NORMAL, mid), mid)
        mid = jnp.where(lo < 0, jnp.where(hi > 0, 0, mid), mid)
        cnt = count_ge(_unsortable(mid))
        ge = cnt >= topk
        act = done < 0.5
        lo = jnp.where(act, jnp.where(ge, mid, lo), lo)
        nlo = jnp.where(act, jnp.where(ge, cnt, nlo), nlo)
        hi = jnp.where(act, jnp.where(ge, hi, mid), hi)
        zero_bracket = jnp.where(lo == 0, jnp.where(hi <= S_MIN_NORMAL, 1.0, 0.0), 0.0)
        done = jnp.maximum(done, jnp.where(nlo == topk, 1.0, jnp.where(hi <= lo + 1, 1.0, zero_bracket)))
        return lo, hi, nlo, done, it + 1, (jnp.min(done) < 0.5).astype(I32)

    lo, hi, nlo, done, _, _ = lax.while_loop(
        cond, step, (lo, hi, jnp.full((nrow, 1), -1.0, F32), done, jnp.int32(0),
                     (jnp.min(done) < 0.5).astype(I32)))
    thr = _unsortable(jnp.where(all_vis, S_NEGINF + 1, lo))
    excess = jnp.where(all_vis, 0.0, nlo - topk)

    @pl.when(jnp.max(excess) > 0.0)
    def _():
        if nrow == SUBLANES:
            _drop_ties(_RowTools(s_sc, rows.start, SUBLANES, nchunks, cw), thr, excess, topk)
            return
        thr_ref, exc_ref = tie_scratch
        thr_ref[...] = jnp.broadcast_to(thr, (nrow, LANES))
        exc_ref[...] = jnp.broadcast_to(excess, (nrow, LANES))

        def group(g, _):
            r0 = pl.multiple_of(g * SUBLANES, SUBLANES)
            exc = exc_ref[pl.ds(r0, SUBLANES), :][:, :1]

            @pl.when(jnp.max(exc) > 0.0)
            def _():
                _drop_ties(_RowTools(s_sc, rows.start + r0, SUBLANES, nchunks, cw),
                           thr_ref[pl.ds(r0, SUBLANES), :][:, :1], exc, topk)
            return 0
        lax.fori_loop(0, nrow // SUBLANES, group, 0)

    return thr


class _RowTools:
    def __init__(self, s_sc, r0, nr, nchunks, cw):
        self.s_sc, self.rows, self.nr, self.nchunks, self.cw = s_sc, pl.ds(r0, nr), nr, nchunks, cw
        self.nl = cw // LANES
        self.lanef = lax.broadcasted_iota(I32, (nr, LANES), 1).astype(F32)

    def blocks(self, c):
        cw, nl = self.cw, self.nl
        off = c * cw if isinstance(c, int) else pl.multiple_of(c * cw, cw)
        blk = self.s_sc[self.rows, pl.ds(off, cw)]
        return [(blk[:, u * LANES:(u + 1) * LANES], off + u * LANES, (c * nl + u) % 2 if nl % 2 else u % 2)
                for u in range(nl)]

    def key_index(self, base):
        return self.lanef + (float(base) if isinstance(base, int) else base.astype(F32))

    def wide(self, v):
        return jnp.broadcast_to(v, (self.nr, LANES))

    def reduce(self, fn, comb, init, lane_red):
        def body(c, acc):
            for b, base, _ in self.blocks(c):
                acc = comb(acc, fn(b, base))
            return acc
        acc = _loop(self.nchunks, body, jnp.full((self.nr, LANES), init, F32))
        return lane_red(acc, axis=1, keepdims=True)

    def erase_where(self, drop_fn):
        def body(c, _):
            for b, base, _ in self.blocks(c):
                self.s_sc[self.rows, pl.ds(base, LANES)] = jnp.where(drop_fn(b, base), -float("inf"), b)
            return 0
        _loop(self.nchunks, body, 0)


def _drop_ties(t, thr, excess, topk):
    inf = float("inf")
    thrb = t.wide(thr)
    many = excess > MANY_TIES
    few = jnp.where(many, 0.0, excess)

    @pl.when(jnp.max(jnp.where(many, 1.0, 0.0)) > 0.0)
    def _():
        need = topk - t.reduce(lambda b, _: jnp.where(b > thrb, 1.0, 0.0), jnp.add, 0.0, jnp.sum)
        nbits = max(1, int(np.ceil(np.log2(t.s_sc.shape[1]))))

        def bis_idx(p, lo_i):
            cand = lo_i + lax.shift_left(jnp.int32(1), nbits - 1 - p)
            cb = t.wide(cand.astype(F32))
            f = t.reduce(lambda b, base: jnp.where((b == thrb) & (t.key_index(base) <= cb), 1.0, 0.0),
                         jnp.add, 0.0, jnp.sum)
            return jnp.where(f < need, cand, lo_i)
        last = (lax.fori_loop(0, nbits, bis_idx, jnp.full(thr.shape, -1, I32)) + 1).astype(F32)
        lb = t.wide(jnp.where(many, last, float(2 ** 30)))
        t.erase_where(lambda b, base: (b == thrb) & (t.key_index(base) > lb))

    nmax = jnp.max(few)

    @pl.when(nmax > 0.0)
    def _():
        def drop(it, _):
            vminb = t.wide(t.reduce(lambda b, _: jnp.where(b >= thrb, b, inf), jnp.minimum, inf, jnp.min))
            last = t.reduce(lambda b, base: jnp.where(b == vminb, t.key_index(base), -1.0),
                            jnp.maximum, -1.0, jnp.max)
            tgtb = t.wide(jnp.where(few > it.astype(F32), last, -2.0))
            t.erase_where(lambda b, base: t.key_index(base) == tgtb)
            return 0
        lax.fori_loop(0, nmax.astype(I32), drop, 0)


def _prompt_select_kernel(qic_ref, wi_ref, kic_ref, bias_ref, s_sc, thr_sc, tie_sc, wb_sc, *, tq, cs, topk):
    q0 = pl.program_id(0) * tq
    nvis = q0 + tq
    nch = (nvis + cs - 1) // cs
    nl = cs // LANES
    for h in range(IDX_HEADS):
        wb_sc[h] = jnp.broadcast_to(wi_ref[:, h:h + 1], (tq, LANES))
    qpos = q0 + lax.broadcasted_iota(I32, (tq, cs), 0)
    kiota = lax.broadcasted_iota(I32, (tq, cs), 1)

    def score_chunk(c, _):
        off = pl.multiple_of(c * cs, cs)
        kc = kic_ref[pl.ds(off, cs), :]
        acc = jnp.zeros((tq, cs), F32)
        for h in range(IDX_HEADS):
            s = _dot_nt(qic_ref[:, h * 256:(h + 1) * 256], kc)
            acc = acc + jnp.maximum(s, 0.0) * _tile_lanes(wb_sc[h], nl)
        s_sc[:, pl.ds(off, cs)] = jnp.where(kiota + off <= qpos, acc, -float("inf"))
        return 0
    lax.fori_loop(0, nch, score_chunk, 0)

    all_vis = q0 + lax.broadcasted_iota(I32, (tq, 1), 0) < topk
    thr = _select_rows(s_sc, slice(0, tq), nch, cs, topk, all_vis, rsub=min(128, tq),
                       tie_scratch=(thr_sc, tie_sc))
    thrb = jnp.broadcast_to(thr, (tq, cs))

    def emit(c, _):
        sl = pl.ds(pl.multiple_of(c * cs, cs), cs)
        bias_ref[:, sl] = jnp.where(s_sc[:, sl] >= thrb, 0.0, NEG).astype(BF16)
        return 0
    lax.fori_loop(0, nch, emit, 0)

    def fill(c, _):
        bias_ref[:, pl.ds(pl.multiple_of(c * cs, cs), cs)] = jnp.full((tq, cs), NEG, BF16)
        return 0
    lax.fori_loop(nch, bias_ref.shape[1] // cs, fill, 0)


def _prompt_select(qic, wi, kic, *, topk, tq, cs):
    t = kic.shape[0]
    qrow = lambda w: pl.BlockSpec((tq, w), lambda i: (i, 0))
    return pl.pallas_call(
        functools.partial(_prompt_select_kernel, tq=tq, cs=cs, topk=topk),
        grid=(t // tq,),
        in_specs=[qrow(qic.shape[1]), qrow(LANES), pl.BlockSpec(kic.shape, lambda i: (0, 0))],
        out_specs=qrow(t),
        out_shape=jax.ShapeDtypeStruct((t, t), BF16),
        scratch_shapes=[
            pltpu.VMEM((tq, t), F32),
            pltpu.VMEM((tq, LANES), F32),
            pltpu.VMEM((tq, LANES), F32),
            pltpu.VMEM((IDX_HEADS, tq, LANES), F32),
        ],
        compiler_params=pltpu.CompilerParams(dimension_semantics=("arbitrary",),
                                             vmem_limit_bytes=VMEM_LIMIT),
        name="prompt_select",
    )(qic, wi, kic)


def _prompt_attn_kernel(qm_ref, kt_ref, vb_ref, bias_ref, o_ref, lg_sc, p_sc, alpha_sc, m_sc, l_sc, acc_sc,
                        *, tq, ts, cs, rb, n_heads):
    i = pl.program_id(0)
    j = pl.program_id(1)
    nkt = pl.num_programs(1)
    nvis = (i + 1) * tq
    nl = cs // LANES

    @pl.when(j == 0)
    def _():
        m_sc[...] = jnp.full(m_sc.shape, NEG, F32)
        l_sc[...] = jnp.zeros(l_sc.shape, F32)
        acc_sc[...] = jnp.zeros(acc_sc.shape, F32)

    @pl.when(j * ts < nvis)
    def _():
        nsub = jnp.minimum((nvis - j * ts + cs - 1) // cs, ts // cs)
        lane = lax.broadcasted_iota(I32, (tq, LANES), 1)
        lo_half = lane < HEAD_DIM

        def sub(u, _):
            loc = pl.multiple_of(u * cs, cs)
            for h in range(n_heads):
                pr = h // 2
                sl = h % 2
                kt = kt_ref[pr * LANES:(pr + 1) * LANES, pl.ds(loc, cs)]
                lg_sc[sl] = _dot(qm_ref[:, h * LANES:(h + 1) * LANES], kt).astype(BF16)
                for r in range(tq // rb):
                    rows = slice(r * rb, (r + 1) * rb)
                    x = lg_sc[sl, rows, :] + bias_ref[rows, pl.ds(loc, cs)]
                    mx = x[:, :LANES]
                    for u2 in range(1, nl):
                        mx = jnp.maximum(mx, x[:, u2 * LANES:(u2 + 1) * LANES])
                    m_prev = m_sc[h, rows, :]
                    m_new = jnp.maximum(m_prev, jnp.max(mx.astype(F32), axis=1, keepdims=True))
                    p = jnp.exp2(x - _tile_lanes(m_new.astype(BF16), nl))
                    ps = p[:, :LANES]
                    for u2 in range(1, nl):
                        ps = ps + p[:, u2 * LANES:(u2 + 1) * LANES]
                    alpha = jnp.exp2(m_prev - m_new)
                    l_sc[h, rows, :] = (alpha * l_sc[h, rows, :]
                                        + jnp.sum(ps.astype(F32), axis=1, keepdims=True))
                    m_sc[h, rows, :] = m_new
                    alpha_sc[sl, rows, :] = alpha
                    p_sc[sl, rows, :] = p
                vt = vb_ref[pl.ds(loc, cs), pr * LANES:(pr + 1) * LANES]
                pv = _dot(p_sc[sl], vt)
                a = acc_sc[pr]
                mine = lo_half if h % 2 == 0 else jnp.logical_not(lo_half)
                acc_sc[pr] = jnp.where(mine, alpha_sc[sl] * a + pv, a)
            return 0
        lax.fori_loop(0, nsub, sub, 0)

    @pl.when(j == nkt - 1)
    def _():
        lane = lax.broadcasted_iota(I32, (tq, LANES), 1)
        lo_half = lane < HEAD_DIM
        for pr in range(n_heads // 2):
            l = jnp.where(lo_half, l_sc[2 * pr], l_sc[2 * pr + 1])
            o_ref[:, pr * LANES:(pr + 1) * LANES] = acc_sc[pr] / l


def _prompt_attn(qm, kt, vb, bias, *, tq, ts, cs):
    t, d_attn = vb.shape
    n_heads = d_attn // HEAD_DIM
    nq, nkt = t // tq, t // ts
    last = lambda i, j: jnp.minimum(j, ((i + 1) * tq - 1) // ts)
    qrow = lambda w: pl.BlockSpec((tq, w), lambda i, j: (i, 0))
    return pl.pallas_call(
        functools.partial(_prompt_attn_kernel, tq=tq, ts=ts, cs=cs, rb=min(32, tq), n_heads=n_heads),
        grid=(nq, nkt),
        in_specs=[qrow(qm.shape[1]),
                  pl.BlockSpec((d_attn, ts), lambda i, j: (0, last(i, j))),
                  pl.BlockSpec((ts, d_attn), lambda i, j: (last(i, j), 0)),
                  pl.BlockSpec((tq, ts), lambda i, j: (i, last(i, j)))],
        out_specs=qrow(d_attn),
        out_shape=jax.ShapeDtypeStruct((t, d_attn), F32),
        scratch_shapes=[
            pltpu.VMEM((2, tq, cs), BF16),
            pltpu.VMEM((2, tq, cs), BF16),
            pltpu.VMEM((2, tq, LANES), F32),
            pltpu.VMEM((n_heads, tq, LANES), F32),
            pltpu.VMEM((n_heads, tq, LANES), F32),
            pltpu.VMEM((n_heads // 2, tq, LANES), F32),
        ],
        compiler_params=pltpu.CompilerParams(dimension_semantics=("arbitrary", "arbitrary"),
                                             vmem_limit_bytes=VMEM_LIMIT),
        name="prompt_attn",
    )(qm, kt, vb, bias)


def _sample_sel_kernel(pt_ref, *refs, ppg, past, topk):
    pages = refs[:ppg]
    qhl_ref, wexp_ref, knh_ref, knl_ref, bias_ref, s_sc = refs[ppg:]
    g = pl.program_id(1)
    ng = pl.num_programs(1)
    nq = bias_ref.shape[0]
    kp = bias_ref.shape[1]
    qhl = qhl_ref[...]
    nr = qhl.shape[0] // 2

    def scores(kh, kl, mm):
        both = mm(qhl, kh)
        s = both[:nr] + both[nr:] + mm(qhl[:nr], kl)
        s = jnp.maximum(s, 0.0) * wexp_ref[...]
        acc = s[0:nq]
        for h in range(1, IDX_HEADS):
            acc = acc + s[h * nq:(h + 1) * nq]
        return acc

    for r in range(ppg):
        kh, kl = _split_bf16(pages[r][...])
        off = pl.multiple_of((g * ppg + r) * PAGE, PAGE)
        s_sc[:, pl.ds(off, PAGE)] = scores(kh, kl, _dot)

    @pl.when(g == ng - 1)
    def _():
        sn = scores(knh_ref[...], knl_ref[...], _dot_nt)
        qi = lax.broadcasted_iota(I32, (nq, PAGE), 0)
        s_sc[:, past:] = jnp.where(lax.broadcasted_iota(I32, (nq, PAGE), 1) <= qi, sn, -float("inf"))
        all_vis = past + lax.broadcasted_iota(I32, (nq, 1), 0) < topk
        thrb = jnp.broadcast_to(
            _select_rows(s_sc, slice(0, nq), kp // PAGE, PAGE, topk, all_vis, rsub=nq), (nq, LANES))
        for c in range(kp // PAGE):
            sl = slice(c * PAGE, (c + 1) * PAGE)
            bias_ref[:, sl] = jnp.where(s_sc[:, sl] >= thrb, 0.0, NEG)


def _sample_sel(page_table, cache_kidx_t, layer, qhl, wexp, knh, knl, *, topk, ppg):
    nb, npages = page_table.shape
    past = npages * PAGE
    kp = past + PAGE
    nq = qhl.shape[1] // (2 * IDX_HEADS)
    ng = npages // ppg

    def page_spec(r):
        return pl.BlockSpec((None, None, IDX_DIM, PAGE),
                            lambda b, g, pt: (layer, pt[b * npages + g * ppg + r], 0, 0))
    per_seq = lambda a: pl.BlockSpec((None,) + a.shape[1:], lambda b, g, pt: (b, 0, 0))
    return pl.pallas_call(
        functools.partial(_sample_sel_kernel, ppg=ppg, past=past, topk=topk),
        grid_spec=pltpu.PrefetchScalarGridSpec(
            num_scalar_prefetch=1,
            grid=(nb, ng),
            in_specs=[page_spec(r) for r in range(ppg)] + [per_seq(a) for a in (qhl, wexp, knh, knl)],
            out_specs=pl.BlockSpec((None, nq, kp), lambda b, g, pt: (b, 0, 0)),
            scratch_shapes=[pltpu.VMEM((nq, kp), F32)],
        ),
        out_shape=jax.ShapeDtypeStruct((nb, nq, kp), F32),
        compiler_params=pltpu.CompilerParams(dimension_semantics=("arbitrary", "arbitrary"),
                                             vmem_limit_bytes=VMEM_LIMIT),
        name="sample_select",
    )(page_table.reshape(-1), *([cache_kidx_t] * ppg), qhl, wexp, knh, knl)


def _sample_attn_kernel(pt_ref, *refs, ppg, n_heads):
    kpages = refs[:ppg]
    vpages = refs[ppg:2 * ppg]
    qbd_ref, bias_ref, biasn_ref, kn_ref, vn_ref, o_ref, m_sc, l_sc, acc_sc = refs[2 * ppg:]
    g = pl.program_id(1)
    ng = pl.num_programs(1)
    nq = o_ref.shape[0]
    d_attn = o_ref.shape[1]
    qbd = qbd_ref[...]

    def heads(b):
        return jnp.concatenate([b] * n_heads, axis=0)

    def update(lg, pv_of):
        m_prev = m_sc[...]
        m_new = jnp.maximum(m_prev, jnp.max(lg, axis=1, keepdims=True))
        alpha = jnp.exp2(m_prev - m_new)
        p = jnp.exp2(lg - m_new[:, :1])
        l_sc[...] = alpha * l_sc[...] + jnp.sum(p, axis=1, keepdims=True)
        m_sc[...] = m_new
        acc_sc[...] = alpha[:, :1] * acc_sc[...] + pv_of(p.astype(BF16))

    @pl.when(g == 0)
    def _():
        m_sc[...] = jnp.full(m_sc.shape, NEG, F32)
        l_sc[...] = jnp.zeros(l_sc.shape, F32)
        acc_sc[...] = jnp.zeros(acc_sc.shape, F32)
        update(_dot_nt(qbd, kn_ref[...]) + heads(biasn_ref[...]), lambda p: _dot(p, vn_ref[...]))

    lg = jnp.concatenate(
        [_dot(qbd, kpages[r][...].reshape(d_attn, PAGE).astype(BF16)) for r in range(ppg)], axis=1)

    def pv_pages(p):
        pv = _dot_nt(p[:, :PAGE], vpages[0][...].reshape(d_attn, PAGE).astype(BF16))
        for r in range(1, ppg):
            pv = pv + _dot_nt(p[:, r * PAGE:(r + 1) * PAGE],
                              vpages[r][...].reshape(d_attn, PAGE).astype(BF16))
        return pv
    update(lg + heads(bias_ref[...]), pv_pages)

    @pl.when(g == ng - 1)
    def _():
        res = acc_sc[...] / l_sc[:, :1]
        lane = lax.broadcasted_iota(I32, (nq, d_attn), 1)
        out = jnp.zeros((nq, d_attn), F32)
        for h in range(n_heads):
            mine = (lane >= h * HEAD_DIM) & (lane < (h + 1) * HEAD_DIM)
            out = jnp.where(mine, res[h * nq:(h + 1) * nq], out)
        o_ref[...] = out


def _sample_attn(page_table, cache_k_t, cache_v_t, layer, qbd, bias, kn, vn, *, ppg):
    nb, npages = page_table.shape
    n_heads = cache_k_t.shape[2]
    d_attn = n_heads * HEAD_DIM
    nq = bias.shape[1]
    ng = npages // ppg

    def page_spec(r):
        return pl.BlockSpec((None, None, n_heads, HEAD_DIM, PAGE),
                            lambda b, g, pt: (layer, pt[b * npages + g * ppg + r], 0, 0, 0))
    per_seq = lambda a: pl.BlockSpec((None,) + a.shape[1:], lambda b, g, pt: (b, 0, 0))
    return pl.pallas_call(
        functools.partial(_sample_attn_kernel, ppg=ppg, n_heads=n_heads),
        grid_spec=pltpu.PrefetchScalarGridSpec(
            num_scalar_prefetch=1,
            grid=(nb, ng),
            in_specs=([page_spec(r) for r in range(ppg)] * 2 + [
                per_seq(qbd),
                pl.BlockSpec((None, nq, ppg * PAGE), lambda b, g, pt: (b, 0, g)),
                pl.BlockSpec((None, nq, PAGE), lambda b, g, pt: (b, 0, npages)),
                per_seq(kn), per_seq(vn)]),
            out_specs=pl.BlockSpec((None, nq, d_attn), lambda b, g, pt: (b, 0, 0)),
            scratch_shapes=[pltpu.VMEM((n_heads * nq, LANES), F32),
                            pltpu.VMEM((n_heads * nq, LANES), F32),
                            pltpu.VMEM((n_heads * nq, d_attn), F32)],
        ),
        out_shape=jax.ShapeDtypeStruct((nb, nq, d_attn), F32),
        compiler_params=pltpu.CompilerParams(dimension_semantics=("arbitrary", "arbitrary"),
                                             vmem_limit_bytes=VMEM_LIMIT),
        name="sample_attn",
    )(page_table.reshape(-1), *([cache_k_t] * ppg), *([cache_v_t] * ppg), qbd, bias, bias, kn, vn)


def _mix_out(y, gc, attn, ga, x, lng, lnb, wo_ref, gpost, d_conv):
    mu = jnp.mean(y, axis=-1, keepdims=True)
    yc = y - mu
    var = jnp.mean(yc * yc, axis=-1, keepdims=True)
    yn = yc * lax.rsqrt(var + EPS) * lng + lnb
    conv_out = _silu(yn) * gc
    o = (_dot(conv_out.astype(BF16), wo_ref[:d_conv, :])
         + _dot((attn * ga).astype(BF16), wo_ref[d_conv:, :]))
    ms = jnp.mean(o * o, axis=-1, keepdims=True)
    return x + o * lax.rsqrt(ms + EPS) * gpost


def _finish_prompt_kernel(glu_ref, halo_ref, gc_ref, attn_ref, ga_ref, x_ref, wdw_ref, bdw_ref,
                          lng_ref, lnb_ref, wo_ref, gpost_ref, y_ref, win_sc, conv_sc, *, rb):
    i = pl.program_id(0)
    tm, d_conv = glu_ref.shape
    halo = halo_ref[...]
    win_sc[:HALO, :] = jnp.where(i == 0, jnp.zeros_like(halo), halo)
    win_sc[HALO:, :] = glu_ref[...]
    base = HALO - (CONV_WIDTH - 1)
    for r in range(tm // rb):
        acc = jnp.broadcast_to(bdw_ref[...], (rb, d_conv))
        for t in range(CONV_WIDTH):
            s = base + t + r * rb
            acc = acc + wdw_ref[t:t + 1, :] * win_sc[s:s + rb, :]
        conv_sc[r * rb:(r + 1) * rb, :] = acc
    y_ref[...] = _mix_out(conv_sc[...], gc_ref[...], attn_ref[...], ga_ref[...], x_ref[...],
                          lng_ref[...], lnb_ref[...], wo_ref, gpost_ref[...], d_conv)


def _finish_prompt(glu, gc, attn, ga, x, w_dw, b_dw, ln_g, ln_b, w_out, g_post, *, tm):
    t, d_conv = glu.shape
    d_model = x.shape[1]
    row = lambda w: pl.BlockSpec((tm, w), lambda i: (i, 0))
    full = lambda a: pl.BlockSpec(a.shape, lambda i: (0, 0))
    return pl.pallas_call(
        functools.partial(_finish_prompt_kernel, rb=32),
        grid=(t // tm,),
        in_specs=[row(d_conv),
                  pl.BlockSpec((HALO, d_conv), lambda i: (jnp.maximum(i * (tm // HALO) - 1, 0), 0)),
                  row(d_conv), row(attn.shape[1]), row(ga.shape[1]), row(d_model),
                  full(w_dw), full(b_dw), full(ln_g), full(ln_b), full(w_out), full(g_post)],
        out_specs=row(d_model),
        out_shape=jax.ShapeDtypeStruct((t, d_model), F32),
        scratch_shapes=[pltpu.VMEM((tm + HALO, d_conv), F32), pltpu.VMEM((tm, d_conv), F32)],
        compiler_params=pltpu.CompilerParams(dimension_semantics=("arbitrary",),
                                             vmem_limit_bytes=VMEM_LIMIT),
        name="finish_prompt",
    )(glu, glu, gc, attn, ga, x, w_dw, b_dw, ln_g, ln_b, w_out, g_post)


def _finish_sample_kernel(hist_ref, gc_ref, attn_ref, ga_ref, x_ref, wdw_ref, bdw_ref,
                          lng_ref, lnb_ref, wo_ref, gpost_ref, y_ref, *, nq):
    nb, _, d_conv = hist_ref.shape
    acc = jnp.broadcast_to(bdw_ref[...].reshape(1, 1, d_conv), (nb, nq, d_conv))
    for t in range(CONV_WIDTH):
        acc = acc + wdw_ref[t:t + 1, :].reshape(1, 1, d_conv) * hist_ref[:, t:t + nq, :]
    y_ref[...] = _mix_out(acc.reshape(nb * nq, d_conv), gc_ref[...], attn_ref[...], ga_ref[...],
                          x_ref[...], lng_ref[...], lnb_ref[...], wo_ref, gpost_ref[...], d_conv)


def _finish_sample(hist, gc, attn, ga, x, w_dw, b_dw, ln_g, ln_b, w_out, g_post, *, nq):
    args = (hist, gc, attn, ga, x, w_dw, b_dw, ln_g, ln_b, w_out, g_post)
    full = lambda a: pl.BlockSpec(a.shape, lambda i: (0,) * a.ndim)
    return pl.pallas_call(
        functools.partial(_finish_sample_kernel, nq=nq),
        grid=(1,),
        in_specs=[full(a) for a in args],
        out_specs=full(x),
        out_shape=jax.ShapeDtypeStruct(x.shape, F32),
        compiler_params=pltpu.CompilerParams(dimension_semantics=("arbitrary",),
                                             vmem_limit_bytes=VMEM_LIMIT),
        name="finish_sample",
    )(*args)


def _tiles(seq):
    tq = min(256, seq)
    tqa = min(512, seq)
    cs = min(512, seq)
    ts = min(2048, seq)
    return tq, tqa, ts, cs


def _layer(layer, xp, xs, cache_k, cache_v, cache_kidx, state, page_table, g_pre, w_in, w_dw, b_dw,
           ln_g, ln_b, w_out, g_post, *, past_len):
    seq, d_model = xp.shape
    nb, nq, _ = xs.shape
    d_mix = w_out.shape[0]
    d_conv = d_mix // 2
    d_attn = d_mix - d_conv
    n_heads = d_attn // HEAD_DIM
    nqi = IDX_HEADS * IDX_DIM
    idx_scale = float(nqi) ** -0.5
    n_main = 3 * d_conv + 4 * d_attn

    w_main = w_in[:, :n_main].astype(BF16)
    zcol = lambda n: jnp.zeros((d_model, n), F32)
    w_idx = jnp.concatenate([w_in[:, n_main:n_main + nqi + IDX_DIM], zcol(LANES - IDX_DIM),
                             w_in[:, n_main + nqi + IDX_DIM:], zcol(LANES - IDX_HEADS)], axis=1)
    w_idx_hi = w_idx.astype(BF16)
    w_idx_lo = (w_idx - w_idx_hi.astype(F32)).astype(BF16)
    w_out_b = w_out.astype(BF16)
    w_dw_p = jnp.concatenate([w_dw, jnp.zeros((1, d_conv), F32)], axis=0)
    row = lambda a: a.reshape(1, -1)
    proj = functools.partial(_project, g_pre=row(g_pre), w_main=w_main, w_idx_hi=w_idx_hi,
                             w_idx_lo=w_idx_lo, d_conv=d_conv, d_attn=d_attn, idx_scale=idx_scale)
    fin_w = (w_dw_p, row(b_dw), row(ln_g), row(ln_b), w_out_b, row(g_post))

    tq, tqa, ts, cs = _tiles(seq)
    (glu, gc, k32, v32, kidx, qm, _, vb, ga, qic, kic, wi, kt) = proj(xp, jnp.arange(seq), tm=min(256, seq))
    bias_p = _prompt_select(qic, wi, kic, topk=min(TOPK_MAX, seq // 4), tq=tq, cs=cs)
    attn = _prompt_attn(qm, kt, vb, bias_p, tq=tqa, ts=ts, cs=cs)
    yp = _finish_prompt(glu, gc, attn, ga, xp, *fin_w, tm=min(256, seq))
    conv_p = glu[seq - (CONV_WIDTH - 1):]

    xs2 = xs.reshape(nb * nq, d_model)
    pos_s = past_len + jnp.tile(jnp.arange(nq), nb)
    (glu_s, gc_s, k32_s, v32_s, kidx_s, qm_s, kb_s, vb_s, ga_s, qic_s, kic_s, wi_s, _) = proj(
        xs2, pos_s, tm=nb * nq)
    by_head = lambda a, nh: a.reshape(nb, nq, nh, -1).transpose(0, 2, 1, 3)
    qic4 = by_head(qic_s, IDX_HEADS).reshape(nb, IDX_HEADS * nq, 256)
    qhl = jnp.concatenate([qic4[..., :IDX_DIM], qic4[..., LANES:LANES + IDX_DIM]], axis=1)
    wexp = jnp.broadcast_to(by_head(wi_s[:, :IDX_HEADS], IDX_HEADS).reshape(nb, IDX_HEADS * nq, 1),
                            (nb, IDX_HEADS * nq, PAGE))
    pad_keys = lambda a: jnp.pad(a, [(0, 0)] * (a.ndim - 2) + [(0, PAGE - nq), (0, 0)])
    knh = pad_keys(kic_s[:, :IDX_DIM].reshape(nb, nq, IDX_DIM))
    knl = pad_keys(kic_s[:, IDX_DIM:2 * IDX_DIM].reshape(nb, nq, IDX_DIM))
    topk_s = min(TOPK_MAX, (past_len + nq) // 4)
    bias = _sample_sel(page_table, cache_kidx.transpose(0, 1, 3, 2), layer, qhl, wexp, knh, knl,
                       topk=topk_s, ppg=8)
    onehot = (jnp.arange(n_heads)[:, None] // 2 == jnp.arange(n_heads // 2)[None, :]).astype(BF16)
    qbd = (by_head(qm_s, n_heads)[:, :, :, None, :]
           * onehot[None, :, None, :, None]).reshape(nb, n_heads * nq, d_attn)
    attn_s = _sample_attn(page_table, cache_k.transpose(0, 1, 3, 4, 2), cache_v.transpose(0, 1, 3, 4, 2),
                          layer, qbd, bias, pad_keys(kb_s.reshape(nb, nq, d_attn)),
                          pad_keys(vb_s.reshape(nb, nq, d_attn)), ppg=8).reshape(nb * nq, d_attn)
    hist_s = jnp.concatenate([state, glu_s.reshape(nb, nq, d_conv),
                              jnp.zeros((nb, 2, d_conv), F32)], axis=1)
    ys = _finish_sample(hist_s, gc_s, attn_s, ga_s, xs2, *fin_w, nq=nq)
    conv_s = hist_s[:, nq:nq + CONV_WIDTH - 1]

    hd = lambda a, lead: a.reshape(lead + (n_heads, HEAD_DIM))
    return (yp, ys.reshape(nb, nq, d_model),
            hd(k32, (1, seq)), hd(v32, (1, seq)), kidx.reshape(1, seq, IDX_DIM), conv_p[None],
            hd(k32_s, (nb, nq)), hd(v32_s, (nb, nq)), kidx_s.reshape(nb, nq, IDX_DIM), conv_s)


def kernel(x_prompt, x_sample, cache_k, cache_v, cache_kidx, state_conv, page_table, g_pre, w_in, w_dw,
           b_dw, ln_g, ln_b, w_out, g_post):
    depth = w_in.shape[0]
    assert x_prompt.shape[0] == 1, "one prompt sequence per call"
    past_len = page_table.shape[1] * PAGE
    xp, xs = x_prompt[0], x_sample
    outs = [[] for _ in range(8)]
    for l in range(depth):
        res = _layer(l, xp, xs, cache_k, cache_v, cache_kidx, state_conv[l], page_table,
                     g_pre[l], w_in[l], w_dw[l], b_dw[l], ln_g[l], ln_b[l], w_out[l], g_post[l],
                     past_len=past_len)
        xp, xs = res[0], res[1]
        for o, r in zip(outs, res[2:]):
            o.append(r)
    return (xp[None], xs) + tuple(jnp.stack(o) for o in outs)
```

```python
import functools
import math

import jax
import jax.numpy as jnp
import numpy as np
from jax import lax
from jax.experimental import pallas as pl
from jax.experimental.pallas import tpu as pltpu

F32 = jnp.float32
BF16 = jnp.bfloat16
I32 = jnp.int32

HEAD_DIM = 64
IDX_DIM = 64
IDX_HEADS = 8
CONV_WIDTH = 31
TOPK_MAX = 256
ROPE_THETA = 500000.0
ROT_DIM = HEAD_DIM // 4
EPS = 1e-6
PAGE = 128
PAGES_PER_STEP = 16
LANES = 128
SUBLANES = 8
HALO = 32
NEG = -2.0 ** 100
INT_MIN = -2 ** 31
S_NEGINF = INT_MIN + 0x7FFFFF
S_POSINF = 0x7F800000
S_MIN_NORMAL = 0x00800000
MANY_TIES = 3
LOG2E = 1.4426950408889634
QSCALE = HEAD_DIM ** -0.5 * LOG2E
VMEM_LIMIT = 56 * 1024 * 1024

_NT = (((1,), (1,)), ((), ()))


def _dot(a, b):
    return jnp.dot(a, b, preferred_element_type=F32)


def _dot_nt(a, b):
    return lax.dot_general(a, b, _NT, preferred_element_type=F32)


def _split_bf16(x):
    hi = x.astype(BF16)
    lo = (x - hi.astype(F32)).astype(BF16)
    return hi, lo


def _flip(b):
    return b ^ (lax.shift_right_arithmetic(b, 31) & 0x7FFFFFFF)


def _sortable(x):
    return _flip(lax.bitcast_convert_type(x, I32))


def _unsortable(s):
    return lax.bitcast_convert_type(_flip(s), F32)


def _loop(n, body, init):
    if isinstance(n, int):
        for c in range(n):
            init = body(c, init)
        return init
    return lax.fori_loop(0, n, body, init)


def _silu(x):
    return x * jax.nn.sigmoid(x)


def _tile_lanes(x, n):
    return x if n == 1 else jnp.concatenate([x] * n, axis=1)


def _proj_kernel(x_ref, g_ref, wm_ref, wih_ref, wil_ref, cos_ref, sin_ref,
                 glu_ref, gc_ref, k32_ref, v32_ref, kidx_ref, qm_ref, kb_ref, vb_ref, ga_ref,
                 qic_ref, kic_ref, wi_ref, kt_ref, *, d_conv, d_attn, idx_scale):
    tm = x_ref.shape[0]
    x = x_ref[...]
    ms = jnp.mean(x * x, axis=-1, keepdims=True)
    xn = x * lax.rsqrt(ms + EPS) * g_ref[...]
    xh, xl = _split_bf16(xn)
    lane = lax.broadcasted_iota(I32, (tm, LANES), 1)
    lo_half = lane < HEAD_DIM
    d = lane & (HEAD_DIM - 1)
    cos = cos_ref[...]
    sin = sin_ref[...]
    sa = jnp.where(d < ROT_DIM // 2, -sin, 0.0)
    sb = jnp.where(d < ROT_DIM // 2, 0.0, sin)

    def rope(c):
        return (c * cos + pltpu.roll(c, LANES - ROT_DIM // 2, 1) * sa
                + pltpu.roll(c, ROT_DIM // 2, 1) * sb)

    def mm(a, b):
        return _dot(xh, wm_ref[:, a:b])

    def mm3(a, b):
        wh = wih_ref[:, a:b]
        return _dot(xh, wh) + _dot(xh, wil_ref[:, a:b]) + _dot(xl, wh)

    o = 0
    ca = mm(o, o + d_conv); o += d_conv
    cb = mm(o, o + d_conv); o += d_conv
    glu_ref[...] = ca * jax.nn.sigmoid(cb)
    cg = mm(o, o + d_conv); o += d_conv
    gc_ref[...] = _silu(cg)
    q = mm(o, o + d_attn); o += d_attn
    for c in range(d_attn // LANES):
        qc = rope(q[:, c * LANES:(c + 1) * LANES]) * QSCALE
        qm_ref[:, (2 * c) * LANES:(2 * c + 1) * LANES] = jnp.where(lo_half, qc, 0.0).astype(BF16)
        qm_ref[:, (2 * c + 1) * LANES:(2 * c + 2) * LANES] = jnp.where(lo_half, 0.0, qc).astype(BF16)
    k = mm(o, o + d_attn); o += d_attn
    for c in range(d_attn // LANES):
        kc = rope(k[:, c * LANES:(c + 1) * LANES])
        k32_ref[:, c * LANES:(c + 1) * LANES] = kc
        kb_ref[:, c * LANES:(c + 1) * LANES] = kc.astype(BF16)
        kt_ref[c * LANES:(c + 1) * LANES, :] = kc.T.astype(BF16)
    v = mm(o, o + d_attn); o += d_attn
    v32_ref[...] = v
    vb_ref[...] = v.astype(BF16)
    ag = mm(o, o + d_attn); o += d_attn
    ga_ref[...] = _silu(ag)

    nqi = IDX_HEADS * IDX_DIM
    qi = mm3(0, nqi)
    for c in range(nqi // LANES):
        t = rope(qi[:, c * LANES:(c + 1) * LANES])
        r = pltpu.roll(t, HEAD_DIM, 1)
        for hh, dup in ((2 * c, jnp.where(lo_half, t, r)), (2 * c + 1, jnp.where(lo_half, r, t))):
            hi = dup.astype(BF16)
            lo = dup - hi.astype(F32)
            qic_ref[:, hh * 256:hh * 256 + LANES] = hi
            qic_ref[:, hh * 256 + LANES:(hh + 1) * 256] = jnp.where(lo_half, lo, 0.0).astype(BF16)
    kw = mm3(nqi, nqi + 2 * LANES)
    kc = rope(kw[:, :LANES])
    kidx_ref[...] = kc[:, :IDX_DIM]
    kk = kc + pltpu.roll(kc, IDX_DIM, 1)
    hi = kk.astype(BF16).astype(F32)
    lo = kk - hi
    kic_ref[:, :LANES] = jnp.where(lo_half, hi, lo).astype(BF16)
    kic_ref[:, LANES:] = jnp.where(lo_half, hi, 0.0).astype(BF16)
    wi_ref[...] = kw[:, LANES:] * idx_scale


def _rope_tables(pos):
    half = ROT_DIM // 2
    d = jnp.arange(LANES) % HEAD_DIM
    inv = ROPE_THETA ** (-(d % half).astype(F32) * 2.0 / ROT_DIM)
    ang = pos.astype(F32)[:, None] * jnp.where(d < ROT_DIM, inv, 0.0)[None, :]
    return jnp.cos(ang), jnp.sin(ang)


def _project(x, pos, g_pre, w_main, w_idx_hi, w_idx_lo, *, d_conv, d_attn, idx_scale, tm):
    t, d_model = x.shape
    cos, sin = _rope_tables(pos)
    row = lambda w: pl.BlockSpec((tm, w), lambda i: (i, 0))
    full = lambda a: pl.BlockSpec(a.shape, lambda i: (0, 0))
    outs = [
        (d_conv, F32), (d_conv, F32), (d_attn, F32), (d_attn, F32), (IDX_DIM, F32),
        (2 * d_attn, BF16), (d_attn, BF16), (d_attn, BF16), (d_attn, F32),
        (IDX_HEADS * 256, BF16), (256, BF16), (LANES, F32),
    ]
    return pl.pallas_call(
        functools.partial(_proj_kernel, d_conv=d_conv, d_attn=d_attn, idx_scale=idx_scale),
        grid=(t // tm,),
        in_specs=[row(d_model), full(g_pre), full(w_main), full(w_idx_hi), full(w_idx_lo),
                  row(LANES), row(LANES)],
        out_specs=[row(w) for w, _ in outs] + [pl.BlockSpec((d_attn, tm), lambda i: (0, i))],
        out_shape=([jax.ShapeDtypeStruct((t, w), dt) for w, dt in outs]
                   + [jax.ShapeDtypeStruct((d_attn, t), BF16)]),
        compiler_params=pltpu.CompilerParams(dimension_semantics=("arbitrary",),
                                             vmem_limit_bytes=VMEM_LIMIT),
        name="project",
    )(x, g_pre, w_main, w_idx_hi, w_idx_lo, cos, sin)


def _select_rows(s_sc, rows, nchunks, cw, topk, all_vis, rsub, tie_scratch=None):
    nrow = rows.stop - rows.start
    nl = cw // LANES
    assert topk <= 2 * LANES and (nl % 2 == 0 or isinstance(nchunks, int)) and nrow % rsub == 0
    inf = float("inf")
    subs = [(slice(r0, r0 + rsub), _RowTools(s_sc, rows.start + r0, rsub, nchunks, cw))
            for r0 in range(0, nrow, rsub)]
    cat = lambda parts: parts[0] if len(parts) == 1 else jnp.concatenate(parts, axis=0)

    def count_ge(v):
        return cat([t.reduce(lambda b, _, vb=t.wide(v[sl]): jnp.where(b >= vb, 1.0, 0.0),
                             jnp.add, 0.0, jnp.sum) for sl, t in subs])

    los, his = [], []
    for _, t in subs:
        def group_max(c, acc, t=t):
            acc = list(acc)
            for b, _, par in t.blocks(c):
                acc[par] = jnp.maximum(acc[par], b)
            return tuple(acc)
        ga, gb = _loop(nchunks, group_max, (jnp.full((rsub, LANES), -inf, F32),) * 2)
        his.append(jnp.max(jnp.maximum(ga, gb), axis=1, keepdims=True))
        los.append(jnp.min(jnp.minimum(ga, gb), axis=1, keepdims=True))
    lo = jnp.maximum(_sortable(cat(los)) - 16, S_NEGINF)
    hi = jnp.minimum(_sortable(cat(his)), S_POSINF - 17) + 17
    done = jnp.where(all_vis, 1.0, 0.0)

    def cond(st):
        return (st[5] > 0) & (st[4] < 64)

    def step(st):
        lo, hi, nlo, done, it, _ = st
        mid = (lax.shift_right_arithmetic(lo, 1) + lax.shift_right_arithmetic(hi, 1) + (lo & hi & 1))
        vmid = _sortable(0.5 * _unsortable(lo) + 0.5 * _unsortable(hi))
        vmid = jnp.minimum(jnp.maximum(vmid, lo + 1), hi - 1)
        far = jnp.where((lo ^ hi) < 0, 1, jnp.where(hi - lo > (1 << 25), 1, 0))
        mid = jnp.where(far * (it < 12).astype(I32) > 0, vmid, mid)
        mid = jnp.where(lo == 0, jnp.where(hi > S_MIN_NORMAL, S_MIN_NORMAL, mid), mid)
        mid = jnp.where(lo < 0, jnp.where(hi > 0, 0, mid), mid)
        cnt = count_ge(_unsortable(mid))
        ge = cnt >= topk
        act = done < 0.5
        lo = jnp.where(act, jnp.where(ge, mid, lo), lo)
        nlo = jnp.where(act, jnp.where(ge, cnt, nlo), nlo)
        hi = jnp.where(act, jnp.where(ge, hi, mid), hi)
        zero_bracket = jnp.where(lo == 0, jnp.where(hi <= S_MIN_NORMAL, 1.0, 0.0), 0.0)
        done = jnp.maximum(done, jnp.where(nlo == topk, 1.0, jnp.where(hi <= lo + 1, 1.0, zero_bracket)))
        return lo, hi, nlo, done, it + 1, (jnp.min(done) < 0.5).astype(I32)

    lo, hi, nlo, done, _, _ = lax.while_loop(
        cond, step, (lo, hi, jnp.full((nrow, 1), -1.0, F32), done, jnp.int32(0),
                     (jnp.min(done) < 0.5).astype(I32)))
    thr = _unsortable(jnp.where(all_vis, S_NEGINF + 1, lo))
    excess = jnp.where(all_vis, 0.0, nlo - topk)

    @pl.when(jnp.max(excess) > 0.0)
    def _():
        if nrow == SUBLANES:
            _drop_ties(_RowTools(s_sc, rows.start, SUBLANES, nchunks, cw), thr, excess, topk)
            return
        thr_ref, exc_ref = tie_scratch
        thr_ref[...] = jnp.broadcast_to(thr, (nrow, LANES))
        exc_ref[...] = jnp.broadcast_to(excess, (nrow, LANES))

        def group(g, _):
            r0 = pl.multiple_of(g * SUBLANES, SUBLANES)
            exc = exc_ref[pl.ds(r0, SUBLANES), :][:, :1]

            @pl.when(jnp.max(exc) > 0.0)
            def _():
                _drop_ties(_RowTools(s_sc, rows.start + r0, SUBLANES, nchunks, cw),
                           thr_ref[pl.ds(r0, SUBLANES), :][:, :1], exc, topk)
            return 0
        lax.fori_loop(0, nrow // SUBLANES, group, 0)

    return thr


class _RowTools:
    def __init__(self, s_sc, r0, nr, nchunks, cw):
        self.s_sc, self.rows, self.nr, self.nchunks, self.cw = s_sc, pl.ds(r0, nr), nr, nchunks, cw
        self.nl = cw // LANES
        self.lanef = lax.broadcasted_iota(I32, (nr, LANES), 1).astype(F32)

    def blocks(self, c):
        cw, nl = self.cw, self.nl
        off = c * cw if isinstance(c, int) else pl.multiple_of(c * cw, cw)
        blk = self.s_sc[self.rows, pl.ds(off, cw)]
        return [(blk[:, u * LANES:(u + 1) * LANES], off + u * LANES, (c * nl + u) % 2 if nl % 2 else u % 2)
                for u in range(nl)]

    def key_index(self, base):
        return self.lanef + (float(base) if isinstance(base, int) else base.astype(F32))

    def wide(self, v):
        return jnp.broadcast_to(v, (self.nr, LANES))

    def reduce(self, fn, comb, init, lane_red):
        def body(c, acc):
            for b, base, _ in self.blocks(c):
                acc = comb(acc, fn(b, base))
            return acc
        acc = _loop(self.nchunks, body, jnp.full((self.nr, LANES), init, F32))
        return lane_red(acc, axis=1, keepdims=True)

    def erase_where(self, drop_fn):
        def body(c, _):
            for b, base, _ in self.blocks(c):
                self.s_sc[self.rows, pl.ds(base, LANES)] = jnp.where(drop_fn(b, base), -float("inf"), b)
            return 0
        _loop(self.nchunks, body, 0)


def _drop_ties(t, thr, excess, topk):
    inf = float("inf")
    thrb = t.wide(thr)
    many = excess > MANY_TIES
    few = jnp.where(many, 0.0, excess)

    @pl.when(jnp.max(jnp.where(many, 1.0, 0.0)) > 0.0)
    def _():
        need = topk - t.reduce(lambda b, _: jnp.where(b > thrb, 1.0, 0.0), jnp.add, 0.0, jnp.sum)
        nbits = max(1, int(np.ceil(np.log2(t.s_sc.shape[1]))))

        def bis_idx(p, lo_i):
            cand = lo_i + lax.shift_left(jnp.int32(1), nbits - 1 - p)
            cb = t.wide(cand.astype(F32))
            f = t.reduce(lambda b, base: jnp.where((b == thrb) & (t.key_index(base) <= cb), 1.0, 0.0),
                         jnp.add, 0.0, jnp.sum)
            return jnp.where(f < need, cand, lo_i)
        last = (lax.fori_loop(0, nbits, bis_idx, jnp.full(thr.shape, -1, I32)) + 1).astype(F32)
        lb = t.wide(jnp.where(many, last, float(2 ** 30)))
        t.erase_where(lambda b, base: (b == thrb) & (t.key_index(base) > lb))

    nmax = jnp.max(few)

    @pl.when(nmax > 0.0)
    def _():
        def drop(it, _):
            vminb = t.wide(t.reduce(lambda b, _: jnp.where(b >= thrb, b, inf), jnp.minimum, inf, jnp.min))
            last = t.reduce(lambda b, base: jnp.where(b == vminb, t.key_index(base), -1.0),
                            jnp.maximum, -1.0, jnp.max)
            tgtb = t.wide(jnp.where(few > it.astype(F32), last, -2.0))
            t.erase_where(lambda b, base: t.key_index(base) == tgtb)
            return 0
        lax.fori_loop(0, nmax.astype(I32), drop, 0)


def _prompt_select_kernel(qic_ref, wi_ref, kic_ref, bias_ref, s_sc, thr_sc, tie_sc, wb_sc, *, tq, cs, topk):
    q0 = pl.program_id(0) * tq
    nvis = q0 + tq
    nch = (nvis + cs - 1) // cs
    nl = cs // LANES
    for h in range(IDX_HEADS):
        wb_sc[h] = jnp.broadcast_to(wi_ref[:, h:h + 1], (tq, LANES))
    qpos = q0 + lax.broadcasted_iota(I32, (tq, cs), 0)
    kiota = lax.broadcasted_iota(I32, (tq, cs), 1)

    def score_chunk(c, _):
        off = pl.multiple_of(c * cs, cs)
        kc = kic_ref[pl.ds(off, cs), :]
        acc = jnp.zeros((tq, cs), F32)
        for h in range(IDX_HEADS):
            s = _dot_nt(qic_ref[:, h * 256:(h + 1) * 256], kc)
            acc = acc + jnp.maximum(s, 0.0) * _tile_lanes(wb_sc[h], nl)
        s_sc[:, pl.ds(off, cs)] = jnp.where(kiota + off <= qpos, acc, -float("inf"))
        return 0
    lax.fori_loop(0, nch, score_chunk, 0)

    all_vis = q0 + lax.broadcasted_iota(I32, (tq, 1), 0) < topk
    thr = _select_rows(s_sc, slice(0, tq), nch, cs, topk, all_vis, rsub=min(128, tq),
                       tie_scratch=(thr_sc, tie_sc))
    thrb = jnp.broadcast_to(thr, (tq, cs))

    def emit(c, _):
        sl = pl.ds(pl.multiple_of(c * cs, cs), cs)
        bias_ref[:, sl] = jnp.where(s_sc[:, sl] >= thrb, 0.0, NEG).astype(BF16)
        return 0
    lax.fori_loop(0, nch, emit, 0)

    def fill(c, _):
        bias_ref[:, pl.ds(pl.multiple_of(c * cs, cs), cs)] = jnp.full((tq, cs), NEG, BF16)
        return 0
    lax.fori_loop(nch, bias_ref.shape[1] // cs, fill, 0)


def _prompt_select(qic, wi, kic, *, topk, tq, cs):
    t = kic.shape[0]
    qrow = lambda w: pl.BlockSpec((tq, w), lambda i: (i, 0))
    return pl.pallas_call(
        functools.partial(_prompt_select_kernel, tq=tq, cs=cs, topk=topk),
        grid=(t // tq,),
        in_specs=[qrow(qic.shape[1]), qrow(LANES), pl.BlockSpec(kic.shape, lambda i: (0, 0))],
        out_specs=qrow(t),
        out_shape=jax.ShapeDtypeStruct((t, t), BF16),
        scratch_shapes=[
            pltpu.VMEM((tq, t), F32),
            pltpu.VMEM((tq, LANES), F32),
            pltpu.VMEM((tq, LANES), F32),
            pltpu.VMEM((IDX_HEADS, tq, LANES), F32),
        ],
        compiler_params=pltpu.CompilerParams(dimension_semantics=("arbitrary",),
                                             vmem_limit_bytes=VMEM_LIMIT),
        name="prompt_select",
    )(qic, wi, kic)


def _prompt_attn_kernel(qm_ref, kt_ref, vb_ref, bias_ref, o_ref, lg_sc, p_sc, alpha_sc, m_sc, l_sc, acc_sc,
                        *, tq, ts, cs, rb, n_heads):
    i = pl.program_id(0)
    j = pl.program_id(1)
    nkt = pl.num_programs(1)
    nvis = (i + 1) * tq
    nl = cs // LANES

    @pl.when(j == 0)
    def _():
        m_sc[...] = jnp.full(m_sc.shape, NEG, F32)
        l_sc[...] = jnp.zeros(l_sc.shape, F32)
        acc_sc[...] = jnp.zeros(acc_sc.shape, F32)

    @pl.when(j * ts < nvis)
    def _():
        nsub = jnp.minimum((nvis - j * ts + cs - 1) // cs, ts // cs)
        lane = lax.broadcasted_iota(I32, (tq, LANES), 1)
        lo_half = lane < HEAD_DIM

        def sub(u, _):
            loc = pl.multiple_of(u * cs, cs)
            for h in range(n_heads):
                pr = h // 2
                sl = h % 2
                kt = kt_ref[pr * LANES:(pr + 1) * LANES, pl.ds(loc, cs)]
                lg_sc[sl] = _dot(qm_ref[:, h * LANES:(h + 1) * LANES], kt).astype(BF16)
                for r in range(tq // rb):
                    rows = slice(r * rb, (r + 1) * rb)
                    x = lg_sc[sl, rows, :] + bias_ref[rows, pl.ds(loc, cs)]
                    mx = x[:, :LANES]
                    for u2 in range(1, nl):
                        mx = jnp.maximum(mx, x[:, u2 * LANES:(u2 + 1) * LANES])
                    m_prev = m_sc[h, rows, :]
                    m_new = jnp.maximum(m_prev, jnp.max(mx.astype(F32), axis=1, keepdims=True))
                    p = jnp.exp2(x - _tile_lanes(m_new.astype(BF16), nl))
                    ps = p[:, :LANES]
                    for u2 in range(1, nl):
                        ps = ps + p[:, u2 * LANES:(u2 + 1) * LANES]
                    alpha = jnp.exp2(m_prev - m_new)
                    l_sc[h, rows, :] = (alpha * l_sc[h, rows, :]
                                        + jnp.sum(ps.astype(F32), axis=1, keepdims=True))
                    m_sc[h, rows, :] = m_new
                    alpha_sc[sl, rows, :] = alpha
                    p_sc[sl, rows, :] = p
                vt = vb_ref[pl.ds(loc, cs), pr * LANES:(pr + 1) * LANES]
                pv = _dot(p_sc[sl], vt)
                a = acc_sc[pr]
                mine = lo_half if h % 2 == 0 else jnp.logical_not(lo_half)
                acc_sc[pr] = jnp.where(mine, alpha_sc[sl] * a + pv, a)
            return 0
        lax.fori_loop(0, nsub, sub, 0)

    @pl.when(j == nkt - 1)
    def _():
        lane = lax.broadcasted_iota(I32, (tq, LANES), 1)
        lo_half = lane < HEAD_DIM
        for pr in range(n_heads // 2):
            l = jnp.where(lo_half, l_sc[2 * pr], l_sc[2 * pr + 1])
            o_ref[:, pr * LANES:(pr + 1) * LANES] = acc_sc[pr] / l


def _prompt_attn(qm, kt, vb, bias, *, tq, ts, cs):
    t, d_attn = vb.shape
    n_heads = d_attn // HEAD_DIM
    nq, nkt = t // tq, t // ts
    last = lambda i, j: jnp.minimum(j, ((i + 1) * tq - 1) // ts)
    qrow = lambda w: pl.BlockSpec((tq, w), lambda i, j: (i, 0))
    return pl.pallas_call(
        functools.partial(_prompt_attn_kernel, tq=tq, ts=ts, cs=cs, rb=min(32, tq), n_heads=n_heads),
        grid=(nq, nkt),
        in_specs=[qrow(qm.shape[1]),
                  pl.BlockSpec((d_attn, ts), lambda i, j: (0, last(i, j))),
                  pl.BlockSpec((ts, d_attn), lambda i, j: (last(i, j), 0)),
                  pl.BlockSpec((tq, ts), lambda i, j: (i, last(i, j)))],
        out_specs=qrow(d_attn),
        out_shape=jax.ShapeDtypeStruct((t, d_attn), F32),
        scratch_shapes=[
            pltpu.VMEM((2, tq, cs), BF16),
            pltpu.VMEM((2, tq, cs), BF16),
            pltpu.VMEM((2, tq, LANES), F32),
            pltpu.VMEM((n_heads, tq, LANES), F32),
            pltpu.VMEM((n_heads, tq, LANES), F32),
            pltpu.VMEM((n_heads // 2, tq, LANES), F32),
        ],
        compiler_params=pltpu.CompilerParams(dimension_semantics=("arbitrary", "arbitrary"),
                                             vmem_limit_bytes=VMEM_LIMIT),
        name="prompt_attn",
    )(qm, kt, vb, bias)


def _sample_sel_kernel(pt_ref, *refs, ppg, past, topk):
    pages = refs[:ppg]
    qhl_ref, wexp_ref, knh_ref, knl_ref, bias_ref, s_sc = refs[ppg:]
    g = pl.program_id(1)
    ng = pl.num_programs(1)
    nq = bias_ref.shape[0]
    kp = bias_ref.shape[1]
    qhl = qhl_ref[...]
    nr = qhl.shape[0] // 2

    def scores(kh, kl, mm):
        both = mm(qhl, kh)
        s = both[:nr] + both[nr:] + mm(qhl[:nr], kl)
        s = jnp.maximum(s, 0.0) * _tile_lanes(wexp_ref[...], s.shape[1] // PAGE)
        acc = s[0:nq]
        for h in range(1, IDX_HEADS):
            acc = acc + s[h * nq:(h + 1) * nq]
        return acc

    kh, kl = _split_bf16(jnp.concatenate([pages[r][...] for r in range(ppg)], axis=1))
    off = pl.multiple_of(g * (ppg * PAGE), ppg * PAGE)
    s_sc[:, pl.ds(off, ppg * PAGE)] = scores(kh, kl, _dot)

    @pl.when(g == ng - 1)
    def _():
        sn = scores(knh_ref[...], knl_ref[...], _dot_nt)
        qi = lax.broadcasted_iota(I32, (nq, PAGE), 0)
        s_sc[:, past:] = jnp.where(lax.broadcasted_iota(I32, (nq, PAGE), 1) <= qi, sn, -float("inf"))
        all_vis = past + lax.broadcasted_iota(I32, (nq, 1), 0) < topk
        thrb = jnp.broadcast_to(
            _select_rows(s_sc, slice(0, nq), kp // PAGE, PAGE, topk, all_vis, rsub=nq), (nq, LANES))
        for c in range(kp // PAGE):
            sl = slice(c * PAGE, (c + 1) * PAGE)
            bias_ref[:, sl] = jnp.where(s_sc[:, sl] >= thrb, 0.0, NEG)


def _sample_sel(page_table, cache_kidx_t, layer, qhl, wexp, knh, knl, *, topk, ppg):
    nb, npages = page_table.shape
    past = npages * PAGE
    kp = past + PAGE
    nq = qhl.shape[1] // (2 * IDX_HEADS)
    ng = npages // ppg

    def page_spec(r):
        return pl.BlockSpec((None, None, IDX_DIM, PAGE),
                            lambda b, g, pt: (layer, pt[b * npages + g * ppg + r], 0, 0))
    per_seq = lambda a: pl.BlockSpec((None,) + a.shape[1:], lambda b, g, pt: (b, 0, 0))
    return pl.pallas_call(
        functools.partial(_sample_sel_kernel, ppg=ppg, past=past, topk=topk),
        grid_spec=pltpu.PrefetchScalarGridSpec(
            num_scalar_prefetch=1,
            grid=(nb, ng),
            in_specs=[page_spec(r) for r in range(ppg)] + [per_seq(a) for a in (qhl, wexp, knh, knl)],
            out_specs=pl.BlockSpec((None, nq, kp), lambda b, g, pt: (b, 0, 0)),
            scratch_shapes=[pltpu.VMEM((nq, kp), F32)],
        ),
        out_shape=jax.ShapeDtypeStruct((nb, nq, kp), F32),
        compiler_params=pltpu.CompilerParams(dimension_semantics=("arbitrary", "arbitrary"),
                                             vmem_limit_bytes=VMEM_LIMIT),
        name="sample_select",
    )(page_table.reshape(-1), *([cache_kidx_t] * ppg), qhl, wexp, knh, knl)


def _sample_attn_kernel(pt_ref, *refs, ppg, n_heads):
    kpages = refs[:ppg]
    vpages = refs[ppg:2 * ppg]
    qbd_ref, bias_ref, biasn_ref, kn_ref, vn_ref, o_ref, m_sc, l_sc, acc_sc = refs[2 * ppg:]
    g = pl.program_id(1)
    ng = pl.num_programs(1)
    nq = o_ref.shape[0]
    d_attn = o_ref.shape[1]
    qbd = qbd_ref[...]

    def heads(b):
        return jnp.concatenate([b] * n_heads, axis=0)

    def update(lg, pv_of):
        m_prev = m_sc[...]
        m_new = jnp.maximum(m_prev, jnp.max(lg, axis=1, keepdims=True))
        alpha = jnp.exp2(m_prev - m_new)
        p = jnp.exp2(lg - m_new[:, :1])
        l_sc[...] = alpha * l_sc[...] + jnp.sum(p, axis=1, keepdims=True)
        m_sc[...] = m_new
        acc_sc[...] = alpha[:, :1] * acc_sc[...] + pv_of(p.astype(BF16))

    @pl.when(g == 0)
    def _():
        m_sc[...] = jnp.full(m_sc.shape, NEG, F32)
        l_sc[...] = jnp.zeros(l_sc.shape, F32)
        acc_sc[...] = jnp.zeros(acc_sc.shape, F32)
        update(_dot_nt(qbd, kn_ref[...]) + heads(biasn_ref[...]), lambda p: _dot(p, vn_ref[...]))

    lg = jnp.concatenate(
        [_dot(qbd, kpages[r][...].reshape(d_attn, PAGE).astype(BF16)) for r in range(ppg)], axis=1)

    def pv_pages(p):
        pv = _dot_nt(p[:, :PAGE], vpages[0][...].reshape(d_attn, PAGE).astype(BF16))
        for r in range(1, ppg):
            pv = pv + _dot_nt(p[:, r * PAGE:(r + 1) * PAGE],
                              vpages[r][...].reshape(d_attn, PAGE).astype(BF16))
        return pv
    update(lg + heads(bias_ref[...]), pv_pages)

    @pl.when(g == ng - 1)
    def _():
        res = acc_sc[...] / l_sc[:, :1]
        lane = lax.broadcasted_iota(I32, (nq, d_attn), 1)
        out = jnp.zeros((nq, d_attn), F32)
        for h in range(n_heads):
            mine = (lane >= h * HEAD_DIM) & (lane < (h + 1) * HEAD_DIM)
            out = jnp.where(mine, res[h * nq:(h + 1) * nq], out)
        o_ref[...] = out


def _sample_attn(page_table, cache_k_t, cache_v_t, layer, qbd, bias, kn, vn, *, ppg):
    nb, npages = page_table.shape
    n_heads = cache_k_t.shape[2]
    d_attn = n_heads * HEAD_DIM
    nq = bias.shape[1]
    ng = npages // ppg

    def page_spec(r):
        return pl.BlockSpec((None, None, n_heads, HEAD_DIM, PAGE),
                            lambda b, g, pt: (layer, pt[b * npages + g * ppg + r], 0, 0, 0))
    per_seq = lambda a: pl.BlockSpec((None,) + a.shape[1:], lambda b, g, pt: (b, 0, 0))
    return pl.pallas_call(
        functools.partial(_sample_attn_kernel, ppg=ppg, n_heads=n_heads),
        grid_spec=pltpu.PrefetchScalarGridSpec(
            num_scalar_prefetch=1,
            grid=(nb, ng),
            in_specs=([page_spec(r) for r in range(ppg)] * 2 + [
                per_seq(qbd),
                pl.BlockSpec((None, nq, ppg * PAGE), lambda b, g, pt: (b, 0, g)),
                pl.BlockSpec((None, nq, PAGE), lambda b, g, pt: (b, 0, npages)),
                per_seq(kn), per_seq(vn)]),
            out_specs=pl.BlockSpec((None, nq, d_attn), lambda b, g, pt: (b, 0, 0)),
            scratch_shapes=[pltpu.VMEM((n_heads * nq, LANES), F32),
                            pltpu.VMEM((n_heads * nq, LANES), F32),
                            pltpu.VMEM((n_heads * nq, d_attn), F32)],
        ),
        out_shape=jax.ShapeDtypeStruct((nb, nq, d_attn), F32),
        compiler_params=pltpu.CompilerParams(dimension_semantics=("arbitrary", "arbitrary"),
                                             vmem_limit_bytes=VMEM_LIMIT),
        name="sample_attn",
    )(page_table.reshape(-1), *([cache_k_t] * ppg), *([cache_v_t] * ppg), qbd, bias, bias, kn, vn)


def _mix_out(y, gc, attn, ga, x, lng, lnb, wo_ref, gpost, d_conv):
    mu = jnp.mean(y, axis=-1, keepdims=True)
    yc = y - mu
    var = jnp.mean(yc * yc, axis=-1, keepdims=True)
    yn = yc * lax.rsqrt(var + EPS) * lng + lnb
    conv_out = _silu(yn) * gc
    o = (_dot(conv_out.astype(BF16), wo_ref[:d_conv, :])
         + _dot((attn * ga).astype(BF16), wo_ref[d_conv:, :]))
    ms = jnp.mean(o * o, axis=-1, keepdims=True)
    return x + o * lax.rsqrt(ms + EPS) * gpost


def _finish_prompt_kernel(glu_ref, halo_ref, gc_ref, attn_ref, ga_ref, x_ref, wdw_ref, bdw_ref,
                          lng_ref, lnb_ref, wo_ref, gpost_ref, y_ref, win_sc, conv_sc, *, rb):
    i = pl.program_id(0)
    tm, d_conv = glu_ref.shape
    halo = halo_ref[...]
    win_sc[:HALO, :] = jnp.where(i == 0, jnp.zeros_like(halo), halo)
    win_sc[HALO:, :] = glu_ref[...]
    base = HALO - (CONV_WIDTH - 1)
    for r in range(tm // rb):
        acc = jnp.broadcast_to(bdw_ref[...], (rb, d_conv))
        for t in range(CONV_WIDTH):
            s = base + t + r * rb
            acc = acc + wdw_ref[t:t + 1, :] * win_sc[s:s + rb, :]
        conv_sc[r * rb:(r + 1) * rb, :] = acc
    y_ref[...] = _mix_out(conv_sc[...], gc_ref[...], attn_ref[...], ga_ref[...], x_ref[...],
                          lng_ref[...], lnb_ref[...], wo_ref, gpost_ref[...], d_conv)


def _finish_prompt(glu, gc, attn, ga, x, w_dw, b_dw, ln_g, ln_b, w_out, g_post, *, tm):
    t, d_conv = glu.shape
    d_model = x.shape[1]
    row = lambda w: pl.BlockSpec((tm, w), lambda i: (i, 0))
    full = lambda a: pl.BlockSpec(a.shape, lambda i: (0, 0))
    return pl.pallas_call(
        functools.partial(_finish_prompt_kernel, rb=32),
        grid=(t // tm,),
        in_specs=[row(d_conv),
                  pl.BlockSpec((HALO, d_conv), lambda i: (jnp.maximum(i * (tm // HALO) - 1, 0), 0)),
                  row(d_conv), row(attn.shape[1]), row(ga.shape[1]), row(d_model),
                  full(w_dw), full(b_dw), full(ln_g), full(ln_b), full(w_out), full(g_post)],
        out_specs=row(d_model),
        out_shape=jax.ShapeDtypeStruct((t, d_model), F32),
        scratch_shapes=[pltpu.VMEM((tm + HALO, d_conv), F32), pltpu.VMEM((tm, d_conv), F32)],
        compiler_params=pltpu.CompilerParams(dimension_semantics=("arbitrary",),
                                             vmem_limit_bytes=VMEM_LIMIT),
        name="finish_prompt",
    )(glu, glu, gc, attn, ga, x, w_dw, b_dw, ln_g, ln_b, w_out, g_post)


def _finish_sample_kernel(hist_ref, gc_ref, attn_ref, ga_ref, x_ref, wdw_ref, bdw_ref,
                          lng_ref, lnb_ref, wo_ref, gpost_ref, y_ref, *, nq):
    nb, _, d_conv = hist_ref.shape
    acc = jnp.broadcast_to(bdw_ref[...].reshape(1, 1, d_conv), (nb, nq, d_conv))
    for t in range(CONV_WIDTH):
        acc = acc + wdw_ref[t:t + 1, :].reshape(1, 1, d_conv) * hist_ref[:, t:t + nq, :]
    y_ref[...] = _mix_out(acc.reshape(nb * nq, d_conv), gc_ref[...], attn_ref[...], ga_ref[...],
                          x_ref[...], lng_ref[...], lnb_ref[...], wo_ref, gpost_ref[...], d_conv)


def _finish_sample(hist, gc, attn, ga, x, w_dw, b_dw, ln_g, ln_b, w_out, g_post, *, nq):
    args = (hist, gc, attn, ga, x, w_dw, b_dw, ln_g, ln_b, w_out, g_post)
    full = lambda a: pl.BlockSpec(a.shape, lambda i: (0,) * a.ndim)
    return pl.pallas_call(
        functools.partial(_finish_sample_kernel, nq=nq),
        grid=(1,),
        in_specs=[full(a) for a in args],
        out_specs=full(x),
        out_shape=jax.ShapeDtypeStruct(x.shape, F32),
        compiler_params=pltpu.CompilerParams(dimension_semantics=("arbitrary",),
                                             vmem_limit_bytes=VMEM_LIMIT),
        name="finish_sample",
    )(*args)


def _tiles(seq):
    tq = min(256, seq)
    tqa = min(512, seq)
    cs = min(512, seq)
    ts = min(2048, seq)
    return tq, tqa, ts, cs


def _layer(layer, xp, xs, cache_k, cache_v, cache_kidx, state, page_table, g_pre, w_in, w_dw, b_dw,
           ln_g, ln_b, w_out, g_post, *, past_len):
    seq, d_model = xp.shape
    nb, nq, _ = xs.shape
    d_mix = w_out.shape[0]
    d_conv = d_mix // 2
    d_attn = d_mix - d_conv
    n_heads = d_attn // HEAD_DIM
    nqi = IDX_HEADS * IDX_DIM
    idx_scale = float(nqi) ** -0.5
    n_main = 3 * d_conv + 4 * d_attn

    w_main = w_in[:, :n_main].astype(BF16)
    zcol = lambda n: jnp.zeros((d_model, n), F32)
    w_idx = jnp.concatenate([w_in[:, n_main:n_main + nqi + IDX_DIM], zcol(LANES - IDX_DIM),
                             w_in[:, n_main + nqi + IDX_DIM:], zcol(LANES - IDX_HEADS)], axis=1)
    w_idx_hi = w_idx.astype(BF16)
    w_idx_lo = (w_idx - w_idx_hi.astype(F32)).astype(BF16)
    w_out_b = w_out.astype(BF16)
    w_dw_p = jnp.concatenate([w_dw, jnp.zeros((1, d_conv), F32)], axis=0)
    row = lambda a: a.reshape(1, -1)
    proj = functools.partial(_project, g_pre=row(g_pre), w_main=w_main, w_idx_hi=w_idx_hi,
                             w_idx_lo=w_idx_lo, d_conv=d_conv, d_attn=d_attn, idx_scale=idx_scale)
    fin_w = (w_dw_p, row(b_dw), row(ln_g), row(ln_b), w_out_b, row(g_post))

    tq, tqa, ts, cs = _tiles(seq)
    (glu, gc, k32, v32, kidx, qm, _, vb, ga, qic, kic, wi, kt) = proj(xp, jnp.arange(seq), tm=min(256, seq))
    bias_p = _prompt_select(qic, wi, kic, topk=min(TOPK_MAX, seq // 4), tq=tq, cs=cs)
    attn = _prompt_attn(qm, kt, vb, bias_p, tq=tqa, ts=ts, cs=cs)
    yp = _finish_prompt(glu, gc, attn, ga, xp, *fin_w, tm=min(256, seq))
    conv_p = glu[seq - (CONV_WIDTH - 1):]

    xs2 = xs.reshape(nb * nq, d_model)
    pos_s = past_len + jnp.tile(jnp.arange(nq), nb)
    (glu_s, gc_s, k32_s, v32_s, kidx_s, qm_s, kb_s, vb_s, ga_s, qic_s, kic_s, wi_s, _) = proj(
        xs2, pos_s, tm=nb * nq)
    by_head = lambda a, nh: a.reshape(nb, nq, nh, -1).transpose(0, 2, 1, 3)
    qic4 = by_head(qic_s, IDX_HEADS).reshape(nb, IDX_HEADS * nq, 256)
    qhl = jnp.concatenate([qic4[..., :IDX_DIM], qic4[..., LANES:LANES + IDX_DIM]], axis=1)
    wexp = jnp.broadcast_to(by_head(wi_s[:, :IDX_HEADS], IDX_HEADS).reshape(nb, IDX_HEADS * nq, 1),
                            (nb, IDX_HEADS * nq, PAGE))
    pad_keys = lambda a: jnp.pad(a, [(0, 0)] * (a.ndim - 2) + [(0, PAGE - nq), (0, 0)])
    knh = pad_keys(kic_s[:, :IDX_DIM].reshape(nb, nq, IDX_DIM))
    knl = pad_keys(kic_s[:, IDX_DIM:2 * IDX_DIM].reshape(nb, nq, IDX_DIM))
    topk_s = min(TOPK_MAX, (past_len + nq) // 4)
    ppg = math.gcd(page_table.shape[1], PAGES_PER_STEP)
    bias = _sample_sel(page_table, cache_kidx.transpose(0, 1, 3, 2), layer, qhl, wexp, knh, knl,
                       topk=topk_s, ppg=ppg)
    onehot = (jnp.arange(n_heads)[:, None] // 2 == jnp.arange(n_heads // 2)[None, :]).astype(BF16)
    qbd = (by_head(qm_s, n_heads)[:, :, :, None, :]
           * onehot[None, :, None, :, None]).reshape(nb, n_heads * nq, d_attn)
    attn_s = _sample_attn(page_table, cache_k.transpose(0, 1, 3, 4, 2), cache_v.transpose(0, 1, 3, 4, 2),
                          layer, qbd, bias, pad_keys(kb_s.reshape(nb, nq, d_attn)),
                          pad_keys(vb_s.reshape(nb, nq, d_attn)), ppg=ppg).reshape(nb * nq, d_attn)
    hist_s = jnp.concatenate([state, glu_s.reshape(nb, nq, d_conv),
                              jnp.zeros((nb, 2, d_conv), F32)], axis=1)
    ys = _finish_sample(hist_s, gc_s, attn_s, ga_s, xs2, *fin_w, nq=nq)
    conv_s = hist_s[:, nq:nq + CONV_WIDTH - 1]

    hd = lambda a, lead: a.reshape(lead + (n_heads, HEAD_DIM))
    return (yp, ys.reshape(nb, nq, d_model),
            hd(k32, (1, seq)), hd(v32, (1, seq)), kidx.reshape(1, seq, IDX_DIM), conv_p[None],
            hd(k32_s, (nb, nq)), hd(v32_s, (nb, nq)), kidx_s.reshape(nb, nq, IDX_DIM), conv_s)


def kernel(x_prompt, x_sample, cache_k, cache_v, cache_kidx, state_conv, page_table, g_pre, w_in, w_dw,
           b_dw, ln_g, ln_b, w_out, g_post):
    depth = w_in.shape[0]
    assert x_prompt.shape[0] == 1, "one prompt sequence per call"
    past_len = page_table.shape[1] * PAGE
    xp, xs = x_prompt[0], x_sample
    outs = [[] for _ in range(8)]
    for l in range(depth):
        res = _layer(l, xp, xs, cache_k, cache_v, cache_kidx, state_conv[l], page_table,
                     g_pre[l], w_in[l], w_dw[l], b_dw[l], ln_g[l], ln_b[l], w_out[l], g_post[l],
                     past_len=past_len)
        xp, xs = res[0], res[1]
        for o, r in zip(outs, res[2:]):
            o.append(r)
    return (xp[None], xs) + tuple(jnp.stack(o) for o in outs)
```

```python
import functools
import math

import jax
import jax.numpy as jnp
import numpy as np
from jax import lax
from jax.experimental import pallas as pl
from jax.experimental.pallas import tpu as pltpu

F32 = jnp.float32
BF16 = jnp.bfloat16
I32 = jnp.int32

HEAD_DIM = 64
IDX_DIM = 64
IDX_HEADS = 8
CONV_WIDTH = 31
TOPK_MAX = 256
ROPE_THETA = 500000.0
ROT_DIM = HEAD_DIM // 4
EPS = 1e-6
PAGE = 128
PAGES_PER_STEP = 16
LANES = 128
SUBLANES = 8
HALO = 32
NEG = -2.0 ** 100
INT_MIN = -2 ** 31
S_NEGINF = INT_MIN + 0x7FFFFF
S_POSINF = 0x7F800000
S_MIN_NORMAL = 0x00800000
MANY_TIES = 3
COARSE_PASSES = 8
LOG2E = 1.4426950408889634
QSCALE = HEAD_DIM ** -0.5 * LOG2E
VMEM_LIMIT = 56 * 1024 * 1024

_NT = (((1,), (1,)), ((), ()))


def _dot(a, b):
    return jnp.dot(a, b, preferred_element_type=F32)


def _dot_nt(a, b):
    return lax.dot_general(a, b, _NT, preferred_element_type=F32)


def _split_bf16(x):
    hi = x.astype(BF16)
    lo = (x - hi.astype(F32)).astype(BF16)
    return hi, lo


def _flip(b):
    return b ^ (lax.shift_right_arithmetic(b, 31) & 0x7FFFFFFF)


def _sortable(x):
    return _flip(lax.bitcast_convert_type(x, I32))


def _unsortable(s):
    return lax.bitcast_convert_type(_flip(s), F32)


def _loop(n, body, init):
    if isinstance(n, int):
        for c in range(n):
            init = body(c, init)
        return init
    return lax.fori_loop(0, n, body, init)


def _silu(x):
    return x * jax.nn.sigmoid(x)


def _tile_lanes(x, n):
    return x if n == 1 else jnp.concatenate([x] * n, axis=1)


def _proj_kernel(x_ref, g_ref, wm_ref, wih_ref, wil_ref, cos_ref, sin_ref,
                 glu_ref, gc_ref, k32_ref, v32_ref, kidx_ref, qm_ref, kb_ref, vb_ref, ga_ref,
                 qic_ref, kic_ref, wi_ref, kt_ref, *, d_conv, d_attn, idx_scale):
    tm = x_ref.shape[0]
    x = x_ref[...]
    ms = jnp.mean(x * x, axis=-1, keepdims=True)
    xn = x * lax.rsqrt(ms + EPS) * g_ref[...]
    xh, xl = _split_bf16(xn)
    lane = lax.broadcasted_iota(I32, (tm, LANES), 1)
    lo_half = lane < HEAD_DIM
    d = lane & (HEAD_DIM - 1)
    cos = cos_ref[...]
    sin = sin_ref[...]
    sa = jnp.where(d < ROT_DIM // 2, -sin, 0.0)
    sb = jnp.where(d < ROT_DIM // 2, 0.0, sin)

    def rope(c):
        return (c * cos + pltpu.roll(c, LANES - ROT_DIM // 2, 1) * sa
                + pltpu.roll(c, ROT_DIM // 2, 1) * sb)

    def mm(a, b):
        return _dot(xh, wm_ref[:, a:b])

    def mm3(a, b):
        wh = wih_ref[:, a:b]
        return _dot(xh, wh) + _dot(xh, wil_ref[:, a:b]) + _dot(xl, wh)

    o = 0
    ca = mm(o, o + d_conv); o += d_conv
    cb = mm(o, o + d_conv); o += d_conv
    glu_ref[...] = ca * jax.nn.sigmoid(cb)
    cg = mm(o, o + d_conv); o += d_conv
    gc_ref[...] = _silu(cg)
    q = mm(o, o + d_attn); o += d_attn
    for c in range(d_attn // LANES):
        qc = rope(q[:, c * LANES:(c + 1) * LANES]) * QSCALE
        qm_ref[:, (2 * c) * LANES:(2 * c + 1) * LANES] = jnp.where(lo_half, qc, 0.0).astype(BF16)
        qm_ref[:, (2 * c + 1) * LANES:(2 * c + 2) * LANES] = jnp.where(lo_half, 0.0, qc).astype(BF16)
    k = mm(o, o + d_attn); o += d_attn
    for c in range(d_attn // LANES):
        kc = rope(k[:, c * LANES:(c + 1) * LANES])
        k32_ref[:, c * LANES:(c + 1) * LANES] = kc
        kb_ref[:, c * LANES:(c + 1) * LANES] = kc.astype(BF16)
        kt_ref[c * LANES:(c + 1) * LANES, :] = kc.T.astype(BF16)
    v = mm(o, o + d_attn); o += d_attn
    v32_ref[...] = v
    vb_ref[...] = v.astype(BF16)
    ag = mm(o, o + d_attn); o += d_attn
    ga_ref[...] = _silu(ag)

    nqi = IDX_HEADS * IDX_DIM
    qi = mm3(0, nqi)
    for c in range(nqi // LANES):
        t = rope(qi[:, c * LANES:(c + 1) * LANES])
        r = pltpu.roll(t, HEAD_DIM, 1)
        for hh, dup in ((2 * c, jnp.where(lo_half, t, r)), (2 * c + 1, jnp.where(lo_half, r, t))):
            hi = dup.astype(BF16)
            lo = dup - hi.astype(F32)
            qic_ref[:, hh * 256:hh * 256 + LANES] = hi
            qic_ref[:, hh * 256 + LANES:(hh + 1) * 256] = jnp.where(lo_half, lo, 0.0).astype(BF16)
    kw = mm3(nqi, nqi + 2 * LANES)
    kc = rope(kw[:, :LANES])
    kidx_ref[...] = kc[:, :IDX_DIM]
    kk = kc + pltpu.roll(kc, IDX_DIM, 1)
    hi = kk.astype(BF16).astype(F32)
    lo = kk - hi
    kic_ref[:, :LANES] = jnp.where(lo_half, hi, lo).astype(BF16)
    kic_ref[:, LANES:] = jnp.where(lo_half, hi, 0.0).astype(BF16)
    wi_ref[...] = kw[:, LANES:] * idx_scale


def _rope_tables(pos):
    half = ROT_DIM // 2
    d = jnp.arange(LANES) % HEAD_DIM
    inv = ROPE_THETA ** (-(d % half).astype(F32) * 2.0 / ROT_DIM)
    ang = pos.astype(F32)[:, None] * jnp.where(d < ROT_DIM, inv, 0.0)[None, :]
    return jnp.cos(ang), jnp.sin(ang)


def _project(x, pos, g_pre, w_main, w_idx_hi, w_idx_lo, *, d_conv, d_attn, idx_scale, tm):
    t, d_model = x.shape
    cos, sin = _rope_tables(pos)
    row = lambda w: pl.BlockSpec((tm, w), lambda i: (i, 0))
    full = lambda a: pl.BlockSpec(a.shape, lambda i: (0, 0))
    outs = [
        (d_conv, F32), (d_conv, F32), (d_attn, F32), (d_attn, F32), (IDX_DIM, F32),
        (2 * d_attn, BF16), (d_attn, BF16), (d_attn, BF16), (d_attn, F32),
        (IDX_HEADS * 256, BF16), (256, BF16), (LANES, F32),
    ]
    return pl.pallas_call(
        functools.partial(_proj_kernel, d_conv=d_conv, d_attn=d_attn, idx_scale=idx_scale),
        grid=(t // tm,),
        in_specs=[row(d_model), full(g_pre), full(w_main), full(w_idx_hi), full(w_idx_lo),
                  row(LANES), row(LANES)],
        out_specs=[row(w) for w, _ in outs] + [pl.BlockSpec((d_attn, tm), lambda i: (0, i))],
        out_shape=([jax.ShapeDtypeStruct((t, w), dt) for w, dt in outs]
                   + [jax.ShapeDtypeStruct((d_attn, t), BF16)]),
        compiler_params=pltpu.CompilerParams(dimension_semantics=("arbitrary",),
                                             vmem_limit_bytes=VMEM_LIMIT),
        name="project",
    )(x, g_pre, w_main, w_idx_hi, w_idx_lo, cos, sin)


def _select_rows(s_sc, rows, nchunks, cw, topk, all_vis, rsub, tie_scratch=None):
    nrow = rows.stop - rows.start
    nl = cw // LANES
    assert topk <= 2 * LANES and (nl % 2 == 0 or isinstance(nchunks, int)) and nrow % rsub == 0
    inf = float("inf")
    subs = [(slice(r0, r0 + rsub), _RowTools(s_sc, rows.start + r0, rsub, nchunks, cw))
            for r0 in range(0, nrow, rsub)]
    cat = lambda parts: parts[0] if len(parts) == 1 else jnp.concatenate(parts, axis=0)

    def count_ge(v):
        return cat([t.reduce(lambda b, _, vb=t.wide(v[sl]): jnp.where(b >= vb, 1.0, 0.0),
                             jnp.add, 0.0, jnp.sum) for sl, t in subs])

    los, his = [], []
    for _, t in subs:
        def group_max(c, acc, t=t):
            acc = list(acc)
            for b, _, par in t.blocks(c):
                acc[par] = jnp.maximum(acc[par], b)
            return tuple(acc)
        ga, gb = _loop(nchunks, group_max, (jnp.full((rsub, LANES), -inf, F32),) * 2)
        his.append(jnp.max(jnp.maximum(ga, gb), axis=1, keepdims=True))
        los.append(jnp.min(jnp.minimum(ga, gb), axis=1, keepdims=True))
    lo = jnp.maximum(_sortable(cat(los)) - 16, S_NEGINF)
    hi = jnp.minimum(_sortable(cat(his)), S_POSINF - 17) + 17
    done = jnp.where(all_vis, 1.0, 0.0)

    def step(st, coarse):
        lo, hi, nlo, done, it, _ = st
        mid = (lax.shift_right_arithmetic(lo, 1) + lax.shift_right_arithmetic(hi, 1) + (lo & hi & 1))
        if coarse:
            vmid = _sortable(0.5 * _unsortable(lo) + 0.5 * _unsortable(hi))
            vmid = jnp.minimum(jnp.maximum(vmid, lo + 1), hi - 1)
            far = jnp.where((lo ^ hi) < 0, 1, jnp.where(hi - lo > (1 << 25), 1, 0))
            mid = jnp.where(far > 0, vmid, mid)
            mid = jnp.where(lo == 0, jnp.where(hi > S_MIN_NORMAL, S_MIN_NORMAL, mid), mid)
            mid = jnp.where(lo < 0, jnp.where(hi > 0, 0, mid), mid)
        cnt = count_ge(_unsortable(mid))
        ge = cnt >= topk
        act = done < 0.5
        lo = jnp.where(act, jnp.where(ge, mid, lo), lo)
        nlo = jnp.where(act, jnp.where(ge, cnt, nlo), nlo)
        hi = jnp.where(act, jnp.where(ge, hi, mid), hi)
        zero_bracket = jnp.where(lo == 0, jnp.where(hi <= S_MIN_NORMAL, 1.0, 0.0), 0.0)
        done = jnp.maximum(done, jnp.where(nlo == topk, 1.0, jnp.where(hi <= lo + 1, 1.0, zero_bracket)))
        return lo, hi, nlo, done, it + 1, (jnp.min(done) < 0.5).astype(I32)

    state = (lo, hi, jnp.full((nrow, 1), -1.0, F32), done, jnp.int32(0), (jnp.min(done) < 0.5).astype(I32))
    state = lax.while_loop(lambda st: (st[5] > 0) & (st[4] < COARSE_PASSES),
                           functools.partial(step, coarse=True), state)
    lo, hi, nlo, done, _, _ = lax.while_loop(lambda st: (st[5] > 0) & (st[4] < 80),
                                             functools.partial(step, coarse=False), state)
    thr = _unsortable(jnp.where(all_vis, S_NEGINF + 1, lo))
    excess = jnp.where(all_vis, 0.0, nlo - topk)

    @pl.when(jnp.max(excess) > 0.0)
    def _():
        if nrow == SUBLANES:
            _drop_ties(_RowTools(s_sc, rows.start, SUBLANES, nchunks, cw), thr, excess, topk)
            return
        thr_ref, exc_ref = tie_scratch
        thr_ref[...] = jnp.broadcast_to(thr, (nrow, LANES))
        exc_ref[...] = jnp.broadcast_to(excess, (nrow, LANES))

        def group(g, _):
            r0 = pl.multiple_of(g * SUBLANES, SUBLANES)
            exc = exc_ref[pl.ds(r0, SUBLANES), :][:, :1]

            @pl.when(jnp.max(exc) > 0.0)
            def _():
                _drop_ties(_RowTools(s_sc, rows.start + r0, SUBLANES, nchunks, cw),
                           thr_ref[pl.ds(r0, SUBLANES), :][:, :1], exc, topk)
            return 0
        lax.fori_loop(0, nrow // SUBLANES, group, 0)

    return thr


class _RowTools:
    def __init__(self, s_sc, r0, nr, nchunks, cw):
        self.s_sc, self.rows, self.nr, self.nchunks, self.cw = s_sc, pl.ds(r0, nr), nr, nchunks, cw
        self.nl = cw // LANES
        self.lanef = lax.broadcasted_iota(I32, (nr, LANES), 1).astype(F32)

    def blocks(self, c):
        cw, nl = self.cw, self.nl
        off = c * cw if isinstance(c, int) else pl.multiple_of(c * cw, cw)
        blk = self.s_sc[self.rows, pl.ds(off, cw)]
        return [(blk[:, u * LANES:(u + 1) * LANES], off + u * LANES, (c * nl + u) % 2 if nl % 2 else u % 2)
                for u in range(nl)]

    def key_index(self, base):
        return self.lanef + (float(base) if isinstance(base, int) else base.astype(F32))

    def wide(self, v):
        return jnp.broadcast_to(v, (self.nr, LANES))

    def reduce(self, fn, comb, init, lane_red):
        def body(c, acc):
            for b, base, _ in self.blocks(c):
                acc = comb(acc, fn(b, base))
            return acc
        acc = _loop(self.nchunks, body, jnp.full((self.nr, LANES), init, F32))
        return lane_red(acc, axis=1, keepdims=True)

    def erase_where(self, drop_fn):
        def body(c, _):
            for b, base, _ in self.blocks(c):
                self.s_sc[self.rows, pl.ds(base, LANES)] = jnp.where(drop_fn(b, base), -float("inf"), b)
            return 0
        _loop(self.nchunks, body, 0)


def _drop_ties(t, thr, excess, topk):
    inf = float("inf")
    thrb = t.wide(thr)
    many = excess > MANY_TIES
    few = jnp.where(many, 0.0, excess)

    @pl.when(jnp.max(jnp.where(many, 1.0, 0.0)) > 0.0)
    def _():
        need = topk - t.reduce(lambda b, _: jnp.where(b > thrb, 1.0, 0.0), jnp.add, 0.0, jnp.sum)
        nbits = max(1, int(np.ceil(np.log2(t.s_sc.shape[1]))))

        def bis_idx(p, lo_i):
            cand = lo_i + lax.shift_left(jnp.int32(1), nbits - 1 - p)
            cb = t.wide(cand.astype(F32))
            f = t.reduce(lambda b, base: jnp.where((b == thrb) & (t.key_index(base) <= cb), 1.0, 0.0),
                         jnp.add, 0.0, jnp.sum)
            return jnp.where(f < need, cand, lo_i)
        last = (lax.fori_loop(0, nbits, bis_idx, jnp.full(thr.shape, -1, I32)) + 1).astype(F32)
        lb = t.wide(jnp.where(many, last, float(2 ** 30)))
        t.erase_where(lambda b, base: (b == thrb) & (t.key_index(base) > lb))

    nmax = jnp.max(few)

    @pl.when(nmax > 0.0)
    def _():
        def drop(it, _):
            vminb = t.wide(t.reduce(lambda b, _: jnp.where(b >= thrb, b, inf), jnp.minimum, inf, jnp.min))
            last = t.reduce(lambda b, base: jnp.where(b == vminb, t.key_index(base), -1.0),
                            jnp.maximum, -1.0, jnp.max)
            tgtb = t.wide(jnp.where(few > it.astype(F32), last, -2.0))
            t.erase_where(lambda b, base: t.key_index(base) == tgtb)
            return 0
        lax.fori_loop(0, nmax.astype(I32), drop, 0)


def _prompt_select_kernel(qic_ref, wi_ref, kic_ref, bias_ref, s_sc, thr_sc, tie_sc, wb_sc, *, tq, cs, topk):
    q0 = pl.program_id(0) * tq
    nvis = q0 + tq
    nch = (nvis + cs - 1) // cs
    nl = cs // LANES
    for h in range(IDX_HEADS):
        wb_sc[h] = jnp.broadcast_to(wi_ref[:, h:h + 1], (tq, LANES))
    qpos = q0 + lax.broadcasted_iota(I32, (tq, cs), 0)
    kiota = lax.broadcasted_iota(I32, (tq, cs), 1)

    def score_chunk(c, _):
        off = pl.multiple_of(c * cs, cs)
        kc = kic_ref[pl.ds(off, cs), :]
        acc = jnp.zeros((tq, cs), F32)
        for h in range(IDX_HEADS):
            s = _dot_nt(qic_ref[:, h * 256:(h + 1) * 256], kc)
            acc = acc + jnp.maximum(s, 0.0) * _tile_lanes(wb_sc[h], nl)
        s_sc[:, pl.ds(off, cs)] = jnp.where(kiota + off <= qpos, acc, -float("inf"))
        return 0
    lax.fori_loop(0, nch, score_chunk, 0)

    all_vis = q0 + lax.broadcasted_iota(I32, (tq, 1), 0) < topk
    thr = _select_rows(s_sc, slice(0, tq), nch, cs, topk, all_vis, rsub=min(128, tq),
                       tie_scratch=(thr_sc, tie_sc))
    thrb = jnp.broadcast_to(thr, (tq, cs))

    def emit(c, _):
        sl = pl.ds(pl.multiple_of(c * cs, cs), cs)
        bias_ref[:, sl] = jnp.where(s_sc[:, sl] >= thrb, 0.0, NEG).astype(BF16)
        return 0
    lax.fori_loop(0, nch, emit, 0)

    def fill(c, _):
        bias_ref[:, pl.ds(pl.multiple_of(c * cs, cs), cs)] = jnp.full((tq, cs), NEG, BF16)
        return 0
    lax.fori_loop(nch, bias_ref.shape[1] // cs, fill, 0)


def _prompt_select(qic, wi, kic, *, topk, tq, cs):
    t = kic.shape[0]
    qrow = lambda w: pl.BlockSpec((tq, w), lambda i: (i, 0))
    return pl.pallas_call(
        functools.partial(_prompt_select_kernel, tq=tq, cs=cs, topk=topk),
        grid=(t // tq,),
        in_specs=[qrow(qic.shape[1]), qrow(LANES), pl.BlockSpec(kic.shape, lambda i: (0, 0))],
        out_specs=qrow(t),
        out_shape=jax.ShapeDtypeStruct((t, t), BF16),
        scratch_shapes=[
            pltpu.VMEM((tq, t), F32),
            pltpu.VMEM((tq, LANES), F32),
            pltpu.VMEM((tq, LANES), F32),
            pltpu.VMEM((IDX_HEADS, tq, LANES), F32),
        ],
        compiler_params=pltpu.CompilerParams(dimension_semantics=("arbitrary",),
                                             vmem_limit_bytes=VMEM_LIMIT),
        name="prompt_select",
    )(qic, wi, kic)


def _prompt_attn_kernel(qm_ref, kt_ref, vb_ref, bias_ref, o_ref, lg_sc, p_sc, alpha_sc, m_sc, l_sc, acc_sc,
                        *, tq, ts, cs, rb, n_heads):
    i = pl.program_id(0)
    j = pl.program_id(1)
    nkt = pl.num_programs(1)
    nvis = (i + 1) * tq
    nl = cs // LANES

    @pl.when(j == 0)
    def _():
        m_sc[...] = jnp.full(m_sc.shape, NEG, F32)
        l_sc[...] = jnp.zeros(l_sc.shape, F32)
        acc_sc[...] = jnp.zeros(acc_sc.shape, F32)

    @pl.when(j * ts < nvis)
    def _():
        nsub = jnp.minimum((nvis - j * ts + cs - 1) // cs, ts // cs)
        lane = lax.broadcasted_iota(I32, (tq, LANES), 1)
        lo_half = lane < HEAD_DIM

        def sub(u, _):
            loc = pl.multiple_of(u * cs, cs)
            for h in range(n_heads):
                pr = h // 2
                sl = h % 2
                kt = kt_ref[pr * LANES:(pr + 1) * LANES, pl.ds(loc, cs)]
                lg_sc[sl] = _dot(qm_ref[:, h * LANES:(h + 1) * LANES], kt).astype(BF16)
                for r in range(tq // rb):
                    rows = slice(r * rb, (r + 1) * rb)
                    x = lg_sc[sl, rows, :] + bias_ref[rows, pl.ds(loc, cs)]
                    mx = x[:, :LANES]
                    for u2 in range(1, nl):
                        mx = jnp.maximum(mx, x[:, u2 * LANES:(u2 + 1) * LANES])
                    m_prev = m_sc[h, rows, :]
                    m_new = jnp.maximum(m_prev, jnp.max(mx.astype(F32), axis=1, keepdims=True))
                    p = jnp.exp2(x - _tile_lanes(m_new.astype(BF16), nl))
                    ps = p[:, :LANES]
                    for u2 in range(1, nl):
                        ps = ps + p[:, u2 * LANES:(u2 + 1) * LANES]
                    alpha = jnp.exp2(m_prev - m_new)
                    l_sc[h, rows, :] = (alpha * l_sc[h, rows, :]
                                        + jnp.sum(ps.astype(F32), axis=1, keepdims=True))
                    m_sc[h, rows, :] = m_new
                    alpha_sc[sl, rows, :] = alpha
                    p_sc[sl, rows, :] = p
                vt = vb_ref[pl.ds(loc, cs), pr * LANES:(pr + 1) * LANES]
                pv = _dot(p_sc[sl], vt)
                a = acc_sc[pr]
                mine = lo_half if h % 2 == 0 else jnp.logical_not(lo_half)
                acc_sc[pr] = jnp.where(mine, alpha_sc[sl] * a + pv, a)
            return 0
        lax.fori_loop(0, nsub, sub, 0)

    @pl.when(j == nkt - 1)
    def _():
        lane = lax.broadcasted_iota(I32, (tq, LANES), 1)
        lo_half = lane < HEAD_DIM
        for pr in range(n_heads // 2):
            l = jnp.where(lo_half, l_sc[2 * pr], l_sc[2 * pr + 1])
            o_ref[:, pr * LANES:(pr + 1) * LANES] = acc_sc[pr] / l


def _prompt_attn(qm, kt, vb, bias, *, tq, ts, cs):
    t, d_attn = vb.shape
    n_heads = d_attn // HEAD_DIM
    nq, nkt = t // tq, t // ts
    last = lambda i, j: jnp.minimum(j, ((i + 1) * tq - 1) // ts)
    qrow = lambda w: pl.BlockSpec((tq, w), lambda i, j: (i, 0))
    return pl.pallas_call(
        functools.partial(_prompt_attn_kernel, tq=tq, ts=ts, cs=cs, rb=min(32, tq), n_heads=n_heads),
        grid=(nq, nkt),
        in_specs=[qrow(qm.shape[1]),
                  pl.BlockSpec((d_attn, ts), lambda i, j: (0, last(i, j))),
                  pl.BlockSpec((ts, d_attn), lambda i, j: (last(i, j), 0)),
                  pl.BlockSpec((tq, ts), lambda i, j: (i, last(i, j)))],
        out_specs=qrow(d_attn),
        out_shape=jax.ShapeDtypeStruct((t, d_attn), F32),
        scratch_shapes=[
            pltpu.VMEM((2, tq, cs), BF16),
            pltpu.VMEM((2, tq, cs), BF16),
            pltpu.VMEM((2, tq, LANES), F32),
            pltpu.VMEM((n_heads, tq, LANES), F32),
            pltpu.VMEM((n_heads, tq, LANES), F32),
            pltpu.VMEM((n_heads // 2, tq, LANES), F32),
        ],
        compiler_params=pltpu.CompilerParams(dimension_semantics=("arbitrary", "arbitrary"),
                                             vmem_limit_bytes=VMEM_LIMIT),
        name="prompt_attn",
    )(qm, kt, vb, bias)


def _sample_sel_kernel(pt_ref, *refs, ppg, past, topk):
    pages = refs[:ppg]
    qhl_ref, wexp_ref, knh_ref, knl_ref, bias_ref, s_sc = refs[ppg:]
    g = pl.program_id(1)
    ng = pl.num_programs(1)
    nq = bias_ref.shape[0]
    kp = bias_ref.shape[1]
    qhl = qhl_ref[...]
    nr = qhl.shape[0] // 2

    def scores(kh, kl, mm):
        both = mm(qhl, kh)
        s = both[:nr] + both[nr:] + mm(qhl[:nr], kl)
        s = jnp.maximum(s, 0.0) * _tile_lanes(wexp_ref[...], s.shape[1] // PAGE)
        acc = s[0:nq]
        for h in range(1, IDX_HEADS):
            acc = acc + s[h * nq:(h + 1) * nq]
        return acc

    kh, kl = _split_bf16(jnp.concatenate([pages[r][...] for r in range(ppg)], axis=1))
    off = pl.multiple_of(g * (ppg * PAGE), ppg * PAGE)
    s_sc[:, pl.ds(off, ppg * PAGE)] = scores(kh, kl, _dot)

    @pl.when(g == ng - 1)
    def _():
        sn = scores(knh_ref[...], knl_ref[...], _dot_nt)
        qi = lax.broadcasted_iota(I32, (nq, PAGE), 0)
        s_sc[:, past:] = jnp.where(lax.broadcasted_iota(I32, (nq, PAGE), 1) <= qi, sn, -float("inf"))
        all_vis = past + lax.broadcasted_iota(I32, (nq, 1), 0) < topk
        thrb = jnp.broadcast_to(
            _select_rows(s_sc, slice(0, nq), kp // PAGE, PAGE, topk, all_vis, rsub=nq), (nq, LANES))
        for c in range(kp // PAGE):
            sl = slice(c * PAGE, (c + 1) * PAGE)
            bias_ref[:, sl] = jnp.where(s_sc[:, sl] >= thrb, 0.0, NEG)


def _sample_sel(page_table, cache_kidx_t, layer, qhl, wexp, knh, knl, *, topk, ppg):
    nb, npages = page_table.shape
    past = npages * PAGE
    kp = past + PAGE
    nq = qhl.shape[1] // (2 * IDX_HEADS)
    ng = npages // ppg

    def page_spec(r):
        return pl.BlockSpec((None, None, IDX_DIM, PAGE),
                            lambda b, g, pt: (layer, pt[b * npages + g * ppg + r], 0, 0))
    per_seq = lambda a: pl.BlockSpec((None,) + a.shape[1:], lambda b, g, pt: (b, 0, 0))
    return pl.pallas_call(
        functools.partial(_sample_sel_kernel, ppg=ppg, past=past, topk=topk),
        grid_spec=pltpu.PrefetchScalarGridSpec(
            num_scalar_prefetch=1,
            grid=(nb, ng),
            in_specs=[page_spec(r) for r in range(ppg)] + [per_seq(a) for a in (qhl, wexp, knh, knl)],
            out_specs=pl.BlockSpec((None, nq, kp), lambda b, g, pt: (b, 0, 0)),
            scratch_shapes=[pltpu.VMEM((nq, kp), F32)],
        ),
        out_shape=jax.ShapeDtypeStruct((nb, nq, kp), F32),
        compiler_params=pltpu.CompilerParams(dimension_semantics=("arbitrary", "arbitrary"),
                                             vmem_limit_bytes=VMEM_LIMIT),
        name="sample_select",
    )(page_table.reshape(-1), *([cache_kidx_t] * ppg), qhl, wexp, knh, knl)


def _sample_attn_kernel(pt_ref, *refs, ppg, n_heads):
    kpages = refs[:ppg]
    vpages = refs[ppg:2 * ppg]
    qbd_ref, bias_ref, biasn_ref, kn_ref, vn_ref, o_ref, m_sc, l_sc, acc_sc = refs[2 * ppg:]
    g = pl.program_id(1)
    ng = pl.num_programs(1)
    nq = o_ref.shape[0]
    d_attn = o_ref.shape[1]
    qbd = qbd_ref[...]

    def heads(b):
        return jnp.concatenate([b] * n_heads, axis=0)

    def update(lg, pv_of):
        m_prev = m_sc[...]
        m_new = jnp.maximum(m_prev, jnp.max(lg, axis=1, keepdims=True))
        alpha = jnp.exp2(m_prev - m_new)
        p = jnp.exp2(lg - m_new[:, :1])
        l_sc[...] = alpha * l_sc[...] + jnp.sum(p, axis=1, keepdims=True)
        m_sc[...] = m_new
        acc_sc[...] = alpha[:, :1] * acc_sc[...] + pv_of(p.astype(BF16))

    @pl.when(g == 0)
    def _():
        m_sc[...] = jnp.full(m_sc.shape, NEG, F32)
        l_sc[...] = jnp.zeros(l_sc.shape, F32)
        acc_sc[...] = jnp.zeros(acc_sc.shape, F32)
        update(_dot_nt(qbd, kn_ref[...]) + heads(biasn_ref[...]), lambda p: _dot(p, vn_ref[...]))

    lg = jnp.concatenate(
        [_dot(qbd, kpages[r][...].reshape(d_attn, PAGE).astype(BF16)) for r in range(ppg)], axis=1)

    def pv_pages(p):
        pv = _dot_nt(p[:, :PAGE], vpages[0][...].reshape(d_attn, PAGE).astype(BF16))
        for r in range(1, ppg):
            pv = pv + _dot_nt(p[:, r * PAGE:(r + 1) * PAGE],
                              vpages[r][...].reshape(d_attn, PAGE).astype(BF16))
        return pv
    update(lg + heads(bias_ref[...]), pv_pages)

    @pl.when(g == ng - 1)
    def _():
        res = acc_sc[...] / l_sc[:, :1]
        lane = lax.broadcasted_iota(I32, (nq, d_attn), 1)
        out = jnp.zeros((nq, d_attn), F32)
        for h in range(n_heads):
            mine = (lane >= h * HEAD_DIM) & (lane < (h + 1) * HEAD_DIM)
            out = jnp.where(mine, res[h * nq:(h + 1) * nq], out)
        o_ref[...] = out


def _sample_attn(page_table, cache_k_t, cache_v_t, layer, qbd, bias, kn, vn, *, ppg):
    nb, npages = page_table.shape
    n_heads = cache_k_t.shape[2]
    d_attn = n_heads * HEAD_DIM
    nq = bias.shape[1]
    ng = npages // ppg

    def page_spec(r):
        return pl.BlockSpec((None, None, n_heads, HEAD_DIM, PAGE),
                            lambda b, g, pt: (layer, pt[b * npages + g * ppg + r], 0, 0, 0))
    per_seq = lambda a: pl.BlockSpec((None,) + a.shape[1:], lambda b, g, pt: (b, 0, 0))
    return pl.pallas_call(
        functools.partial(_sample_attn_kernel, ppg=ppg, n_heads=n_heads),
        grid_spec=pltpu.PrefetchScalarGridSpec(
            num_scalar_prefetch=1,
            grid=(nb, ng),
            in_specs=([page_spec(r) for r in range(ppg)] * 2 + [
                per_seq(qbd),
                pl.BlockSpec((None, nq, ppg * PAGE), lambda b, g, pt: (b, 0, g)),
                pl.BlockSpec((None, nq, PAGE), lambda b, g, pt: (b, 0, npages)),
                per_seq(kn), per_seq(vn)]),
            out_specs=pl.BlockSpec((None, nq, d_attn), lambda b, g, pt: (b, 0, 0)),
            scratch_shapes=[pltpu.VMEM((n_heads * nq, LANES), F32),
                            pltpu.VMEM((n_heads * nq, LANES), F32),
                            pltpu.VMEM((n_heads * nq, d_attn), F32)],
        ),
        out_shape=jax.ShapeDtypeStruct((nb, nq, d_attn), F32),
        compiler_params=pltpu.CompilerParams(dimension_semantics=("arbitrary", "arbitrary"),
                                             vmem_limit_bytes=VMEM_LIMIT),
        name="sample_attn",
    )(page_table.reshape(-1), *([cache_k_t] * ppg), *([cache_v_t] * ppg), qbd, bias, bias, kn, vn)


def _mix_out(y, gc, attn, ga, x, lng, lnb, wo_ref, gpost, d_conv):
    mu = jnp.mean(y, axis=-1, keepdims=True)
    yc = y - mu
    var = jnp.mean(yc * yc, axis=-1, keepdims=True)
    yn = yc * lax.rsqrt(var + EPS) * lng + lnb
    conv_out = _silu(yn) * gc
    o = (_dot(conv_out.astype(BF16), wo_ref[:d_conv, :])
         + _dot((attn * ga).astype(BF16), wo_ref[d_conv:, :]))
    ms = jnp.mean(o * o, axis=-1, keepdims=True)
    return x + o * lax.rsqrt(ms + EPS) * gpost


def _finish_prompt_kernel(glu_ref, halo_ref, gc_ref, attn_ref, ga_ref, x_ref, wdw_ref, bdw_ref,
                          lng_ref, lnb_ref, wo_ref, gpost_ref, y_ref, win_sc, conv_sc, *, rb):
    i = pl.program_id(0)
    tm, d_conv = glu_ref.shape
    halo = halo_ref[...]
    win_sc[:HALO, :] = jnp.where(i == 0, jnp.zeros_like(halo), halo)
    win_sc[HALO:, :] = glu_ref[...]
    base = HALO - (CONV_WIDTH - 1)
    for r in range(tm // rb):
        acc = jnp.broadcast_to(bdw_ref[...], (rb, d_conv))
        for t in range(CONV_WIDTH):
            s = base + t + r * rb
            acc = acc + wdw_ref[t:t + 1, :] * win_sc[s:s + rb, :]
        conv_sc[r * rb:(r + 1) * rb, :] = acc
    y_ref[...] = _mix_out(conv_sc[...], gc_ref[...], attn_ref[...], ga_ref[...], x_ref[...],
                          lng_ref[...], lnb_ref[...], wo_ref, gpost_ref[...], d_conv)


def _finish_prompt(glu, gc, attn, ga, x, w_dw, b_dw, ln_g, ln_b, w_out, g_post, *, tm):
    t, d_conv = glu.shape
    d_model = x.shape[1]
    row = lambda w: pl.BlockSpec((tm, w), lambda i: (i, 0))
    full = lambda a: pl.BlockSpec(a.shape, lambda i: (0, 0))
    return pl.pallas_call(
        functools.partial(_finish_prompt_kernel, rb=32),
        grid=(t // tm,),
        in_specs=[row(d_conv),
                  pl.BlockSpec((HALO, d_conv), lambda i: (jnp.maximum(i * (tm // HALO) - 1, 0), 0)),
                  row(d_conv), row(attn.shape[1]), row(ga.shape[1]), row(d_model),
                  full(w_dw), full(b_dw), full(ln_g), full(ln_b), full(w_out), full(g_post)],
        out_specs=row(d_model),
        out_shape=jax.ShapeDtypeStruct((t, d_model), F32),
        scratch_shapes=[pltpu.VMEM((tm + HALO, d_conv), F32), pltpu.VMEM((tm, d_conv), F32)],
        compiler_params=pltpu.CompilerParams(dimension_semantics=("arbitrary",),
                                             vmem_limit_bytes=VMEM_LIMIT),
        name="finish_prompt",
    )(glu, glu, gc, attn, ga, x, w_dw, b_dw, ln_g, ln_b, w_out, g_post)


def _finish_sample_kernel(hist_ref, gc_ref, attn_ref, ga_ref, x_ref, wdw_ref, bdw_ref,
                          lng_ref, lnb_ref, wo_ref, gpost_ref, y_ref, *, nq):
    nb, _, d_conv = hist_ref.shape
    acc = jnp.broadcast_to(bdw_ref[...].reshape(1, 1, d_conv), (nb, nq, d_conv))
    for t in range(CONV_WIDTH):
        acc = acc + wdw_ref[t:t + 1, :].reshape(1, 1, d_conv) * hist_ref[:, t:t + nq, :]
    y_ref[...] = _mix_out(acc.reshape(nb * nq, d_conv), gc_ref[...], attn_ref[...], ga_ref[...],
                          x_ref[...], lng_ref[...], lnb_ref[...], wo_ref, gpost_ref[...], d_conv)


def _finish_sample(hist, gc, attn, ga, x, w_dw, b_dw, ln_g, ln_b, w_out, g_post, *, nq):
    args = (hist, gc, attn, ga, x, w_dw, b_dw, ln_g, ln_b, w_out, g_post)
    full = lambda a: pl.BlockSpec(a.shape, lambda i: (0,) * a.ndim)
    return pl.pallas_call(
        functools.partial(_finish_sample_kernel, nq=nq),
        grid=(1,),
        in_specs=[full(a) for a in args],
        out_specs=full(x),
        out_shape=jax.ShapeDtypeStruct(x.shape, F32),
        compiler_params=pltpu.CompilerParams(dimension_semantics=("arbitrary",),
                                             vmem_limit_bytes=VMEM_LIMIT),
        name="finish_sample",
    )(*args)


def _tiles(seq):
    tq = min(256, seq)
    tqa = min(1024, seq)
    cs = min(512, seq)
    ts = min(2048, seq)
    return tq, tqa, ts, cs


def _layer(layer, xp, xs, cache_k, cache_v, cache_kidx, state, page_table, g_pre, w_in, w_dw, b_dw,
           ln_g, ln_b, w_out, g_post, *, past_len):
    seq, d_model = xp.shape
    nb, nq, _ = xs.shape
    d_mix = w_out.shape[0]
    d_conv = d_mix // 2
    d_attn = d_mix - d_conv
    n_heads = d_attn // HEAD_DIM
    nqi = IDX_HEADS * IDX_DIM
    idx_scale = float(nqi) ** -0.5
    n_main = 3 * d_conv + 4 * d_attn

    w_main = w_in[:, :n_main].astype(BF16)
    zcol = lambda n: jnp.zeros((d_model, n), F32)
    w_idx = jnp.concatenate([w_in[:, n_main:n_main + nqi + IDX_DIM], zcol(LANES - IDX_DIM),
                             w_in[:, n_main + nqi + IDX_DIM:], zcol(LANES - IDX_HEADS)], axis=1)
    w_idx_hi = w_idx.astype(BF16)
    w_idx_lo = (w_idx - w_idx_hi.astype(F32)).astype(BF16)
    w_out_b = w_out.astype(BF16)
    w_dw_p = jnp.concatenate([w_dw, jnp.zeros((1, d_conv), F32)], axis=0)
    row = lambda a: a.reshape(1, -1)
    proj = functools.partial(_project, g_pre=row(g_pre), w_main=w_main, w_idx_hi=w_idx_hi,
                             w_idx_lo=w_idx_lo, d_conv=d_conv, d_attn=d_attn, idx_scale=idx_scale)
    fin_w = (w_dw_p, row(b_dw), row(ln_g), row(ln_b), w_out_b, row(g_post))

    tq, tqa, ts, cs = _tiles(seq)
    (glu, gc, k32, v32, kidx, qm, _, vb, ga, qic, kic, wi, kt) = proj(xp, jnp.arange(seq), tm=min(256, seq))
    bias_p = _prompt_select(qic, wi, kic, topk=min(TOPK_MAX, seq // 4), tq=tq, cs=cs)
    attn = _prompt_attn(qm, kt, vb, bias_p, tq=tqa, ts=ts, cs=cs)
    yp = _finish_prompt(glu, gc, attn, ga, xp, *fin_w, tm=min(256, seq))
    conv_p = glu[seq - (CONV_WIDTH - 1):]

    xs2 = xs.reshape(nb * nq, d_model)
    pos_s = past_len + jnp.tile(jnp.arange(nq), nb)
    (glu_s, gc_s, k32_s, v32_s, kidx_s, qm_s, kb_s, vb_s, ga_s, qic_s, kic_s, wi_s, _) = proj(
        xs2, pos_s, tm=nb * nq)
    by_head = lambda a, nh: a.reshape(nb, nq, nh, -1).transpose(0, 2, 1, 3)
    qic4 = by_head(qic_s, IDX_HEADS).reshape(nb, IDX_HEADS * nq, 256)
    qhl = jnp.concatenate([qic4[..., :IDX_DIM], qic4[..., LANES:LANES + IDX_DIM]], axis=1)
    wexp = jnp.broadcast_to(by_head(wi_s[:, :IDX_HEADS], IDX_HEADS).reshape(nb, IDX_HEADS * nq, 1),
                            (nb, IDX_HEADS * nq, PAGE))
    pad_keys = lambda a: jnp.pad(a, [(0, 0)] * (a.ndim - 2) + [(0, PAGE - nq), (0, 0)])
    knh = pad_keys(kic_s[:, :IDX_DIM].reshape(nb, nq, IDX_DIM))
    knl = pad_keys(kic_s[:, IDX_DIM:2 * IDX_DIM].reshape(nb, nq, IDX_DIM))
    topk_s = min(TOPK_MAX, (past_len + nq) // 4)
    ppg = math.gcd(page_table.shape[1], PAGES_PER_STEP)
    bias = _sample_sel(page_table, cache_kidx.transpose(0, 1, 3, 2), layer, qhl, wexp, knh, knl,
                       topk=topk_s, ppg=ppg)
    onehot = (jnp.arange(n_heads)[:, None] // 2 == jnp.arange(n_heads // 2)[None, :]).astype(BF16)
    qbd = (by_head(qm_s, n_heads)[:, :, :, None, :]
           * onehot[None, :, None, :, None]).reshape(nb, n_heads * nq, d_attn)
    attn_s = _sample_attn(page_table, cache_k.transpose(0, 1, 3, 4, 2), cache_v.transpose(0, 1, 3, 4, 2),
                          layer, qbd, bias, pad_keys(kb_s.reshape(nb, nq, d_attn)),
                          pad_keys(vb_s.reshape(nb, nq, d_attn)), ppg=ppg).reshape(nb * nq, d_attn)
    hist_s = jnp.concatenate([state, glu_s.reshape(nb, nq, d_conv),
                              jnp.zeros((nb, 2, d_conv), F32)], axis=1)
    ys = _finish_sample(hist_s, gc_s, attn_s, ga_s, xs2, *fin_w, nq=nq)
    conv_s = hist_s[:, nq:nq + CONV_WIDTH - 1]

    hd = lambda a, lead: a.reshape(lead + (n_heads, HEAD_DIM))
    return (yp, ys.reshape(nb, nq, d_model),
            hd(k32, (1, seq)), hd(v32, (1, seq)), kidx.reshape(1, seq, IDX_DIM), conv_p[None],
            hd(k32_s, (nb, nq)), hd(v32_s, (nb, nq)), kidx_s.reshape(nb, nq, IDX_DIM), conv_s)


def kernel(x_prompt, x_sample, cache_k, cache_v, cache_kidx, state_conv, page_table, g_pre, w_in, w_dw,
           b_dw, ln_g, ln_b, w_out, g_post):
    depth = w_in.shape[0]
    assert x_prompt.shape[0] == 1, "one prompt sequence per call"
    past_len = page_table.shape[1] * PAGE
    xp, xs = x_prompt[0], x_sample
    outs = [[] for _ in range(8)]
    for l in range(depth):
        res = _layer(l, xp, xs, cache_k, cache_v, cache_kidx, state_conv[l], page_table,
                     g_pre[l], w_in[l], w_dw[l], b_dw[l], ln_g[l], ln_b[l], w_out[l], g_post[l],
                     past_len=past_len)
        xp, xs = res[0], res[1]
        for o, r in zip(outs, res[2:]):
            o.append(r)
    return (xp[None], xs) + tuple(jnp.stack(o) for o in outs)
```

```python
import functools
import math

import jax
import jax.numpy as jnp
import numpy as np
from jax import lax
from jax.experimental import pallas as pl
from jax.experimental.pallas import tpu as pltpu

F32 = jnp.float32
BF16 = jnp.bfloat16
I32 = jnp.int32

HEAD_DIM = 64
IDX_DIM = 64
IDX_HEADS = 8
CONV_WIDTH = 31
TOPK_MAX = 256
ROPE_THETA = 500000.0
ROT_DIM = HEAD_DIM // 4
EPS = 1e-6
PAGE = 128
PAGES_PER_STEP = 16
LANES = 128
SUBLANES = 8
HALO = 32
NEG = -2.0 ** 100
INT_MIN = -2 ** 31
S_NEGINF = INT_MIN + 0x7FFFFF
S_POSINF = 0x7F800000
S_MIN_NORMAL = 0x00800000
MANY_TIES = 3
COARSE_PASSES = 8
LOG2E = 1.4426950408889634
QSCALE = HEAD_DIM ** -0.5 * LOG2E
VMEM_LIMIT = 56 * 1024 * 1024

_NT = (((1,), (1,)), ((), ()))


def _dot(a, b):
    return jnp.dot(a, b, preferred_element_type=F32)


def _dot_nt(a, b):
    return lax.dot_general(a, b, _NT, preferred_element_type=F32)


def _split_bf16(x):
    hi = x.astype(BF16)
    lo = (x - hi.astype(F32)).astype(BF16)
    return hi, lo


def _flip(b):
    return b ^ (lax.shift_right_arithmetic(b, 31) & 0x7FFFFFFF)


def _sortable(x):
    return _flip(lax.bitcast_convert_type(x, I32))


def _unsortable(s):
    return lax.bitcast_convert_type(_flip(s), F32)


def _loop(n, body, init):
    if isinstance(n, int):
        for c in range(n):
            init = body(c, init)
        return init
    return lax.fori_loop(0, n, body, init)


def _silu(x):
    return x * jax.nn.sigmoid(x)


def _tile_lanes(x, n):
    return x if n == 1 else jnp.concatenate([x] * n, axis=1)


def _proj_kernel(x_ref, g_ref, wm_ref, wih_ref, wil_ref, cos_ref, sin_ref,
                 glu_ref, gc_ref, k32_ref, v32_ref, kidx_ref, qm_ref, kb_ref, vb_ref, ga_ref,
                 qic_ref, kic_ref, wi_ref, kt_ref, *, d_conv, d_attn, idx_scale):
    tm = x_ref.shape[0]
    x = x_ref[...]
    ms = jnp.mean(x * x, axis=-1, keepdims=True)
    xn = x * lax.rsqrt(ms + EPS) * g_ref[...]
    xh, xl = _split_bf16(xn)
    lane = lax.broadcasted_iota(I32, (tm, LANES), 1)
    lo_half = lane < HEAD_DIM
    d = lane & (HEAD_DIM - 1)
    cos = cos_ref[...]
    sin = sin_ref[...]
    sa = jnp.where(d < ROT_DIM // 2, -sin, 0.0)
    sb = jnp.where(d < ROT_DIM // 2, 0.0, sin)

    def rope(c):
        return (c * cos + pltpu.roll(c, LANES - ROT_DIM // 2, 1) * sa
                + pltpu.roll(c, ROT_DIM // 2, 1) * sb)

    def mm(a, b):
        return _dot(xh, wm_ref[:, a:b])

    def mm3(a, b):
        wh = wih_ref[:, a:b]
        return _dot(xh, wh) + _dot(xh, wil_ref[:, a:b]) + _dot(xl, wh)

    o = 0
    ca = mm(o, o + d_conv); o += d_conv
    cb = mm(o, o + d_conv); o += d_conv
    glu_ref[...] = ca * jax.nn.sigmoid(cb)
    cg = mm(o, o + d_conv); o += d_conv
    gc_ref[...] = _silu(cg)
    q = mm(o, o + d_attn); o += d_attn
    for c in range(d_attn // LANES):
        qc = rope(q[:, c * LANES:(c + 1) * LANES]) * QSCALE
        qm_ref[:, (2 * c) * LANES:(2 * c + 1) * LANES] = jnp.where(lo_half, qc, 0.0).astype(BF16)
        qm_ref[:, (2 * c + 1) * LANES:(2 * c + 2) * LANES] = jnp.where(lo_half, 0.0, qc).astype(BF16)
    k = mm(o, o + d_attn); o += d_attn
    for c in range(d_attn // LANES):
        kc = rope(k[:, c * LANES:(c + 1) * LANES])
        k32_ref[:, c * LANES:(c + 1) * LANES] = kc
        kb_ref[:, c * LANES:(c + 1) * LANES] = kc.astype(BF16)
        kt_ref[c * LANES:(c + 1) * LANES, :] = kc.T.astype(BF16)
    v = mm(o, o + d_attn); o += d_attn
    v32_ref[...] = v
    vb_ref[...] = v.astype(BF16)
    ag = mm(o, o + d_attn); o += d_attn
    ga_ref[...] = _silu(ag)

    nqi = IDX_HEADS * IDX_DIM
    qi = mm3(0, nqi)
    for c in range(nqi // LANES):
        t = rope(qi[:, c * LANES:(c + 1) * LANES])
        r = pltpu.roll(t, HEAD_DIM, 1)
        for hh, dup in ((2 * c, jnp.where(lo_half, t, r)), (2 * c + 1, jnp.where(lo_half, r, t))):
            hi = dup.astype(BF16)
            lo = dup - hi.astype(F32)
            qic_ref[:, hh * 256:hh * 256 + LANES] = hi
            qic_ref[:, hh * 256 + LANES:(hh + 1) * 256] = jnp.where(lo_half, lo, 0.0).astype(BF16)
    kw = mm3(nqi, nqi + 2 * LANES)
    kc = rope(kw[:, :LANES])
    kidx_ref[...] = kc[:, :IDX_DIM]
    kk = kc + pltpu.roll(kc, IDX_DIM, 1)
    hi = kk.astype(BF16).astype(F32)
    lo = kk - hi
    kic_ref[:, :LANES] = jnp.where(lo_half, hi, lo).astype(BF16)
    kic_ref[:, LANES:] = jnp.where(lo_half, hi, 0.0).astype(BF16)
    wi_ref[...] = kw[:, LANES:] * idx_scale


def _rope_tables(pos):
    half = ROT_DIM // 2
    d = jnp.arange(LANES) % HEAD_DIM
    inv = ROPE_THETA ** (-(d % half).astype(F32) * 2.0 / ROT_DIM)
    ang = pos.astype(F32)[:, None] * jnp.where(d < ROT_DIM, inv, 0.0)[None, :]
    return jnp.cos(ang), jnp.sin(ang)


def _project(x, pos, g_pre, w_main, w_idx_hi, w_idx_lo, *, d_conv, d_attn, idx_scale, tm):
    t, d_model = x.shape
    cos, sin = _rope_tables(pos)
    row = lambda w: pl.BlockSpec((tm, w), lambda i: (i, 0))
    full = lambda a: pl.BlockSpec(a.shape, lambda i: (0, 0))
    outs = [
        (d_conv, F32), (d_conv, F32), (d_attn, F32), (d_attn, F32), (IDX_DIM, F32),
        (2 * d_attn, BF16), (d_attn, BF16), (d_attn, BF16), (d_attn, F32),
        (IDX_HEADS * 256, BF16), (256, BF16), (LANES, F32),
    ]
    return pl.pallas_call(
        functools.partial(_proj_kernel, d_conv=d_conv, d_attn=d_attn, idx_scale=idx_scale),
        grid=(t // tm,),
        in_specs=[row(d_model), full(g_pre), full(w_main), full(w_idx_hi), full(w_idx_lo),
                  row(LANES), row(LANES)],
        out_specs=[row(w) for w, _ in outs] + [pl.BlockSpec((d_attn, tm), lambda i: (0, i))],
        out_shape=([jax.ShapeDtypeStruct((t, w), dt) for w, dt in outs]
                   + [jax.ShapeDtypeStruct((d_attn, t), BF16)]),
        compiler_params=pltpu.CompilerParams(dimension_semantics=("arbitrary",),
                                             vmem_limit_bytes=VMEM_LIMIT),
        name="project",
    )(x, g_pre, w_main, w_idx_hi, w_idx_lo, cos, sin)


def _select_rows(s_sc, rows, nchunks, cw, topk, all_vis, rsub, tie_scratch=None):
    nrow = rows.stop - rows.start
    nl = cw // LANES
    assert topk <= 2 * LANES and (nl % 2 == 0 or isinstance(nchunks, int)) and nrow % rsub == 0
    inf = float("inf")
    subs = [(slice(r0, r0 + rsub), _RowTools(s_sc, rows.start + r0, rsub, nchunks, cw))
            for r0 in range(0, nrow, rsub)]
    cat = lambda parts: parts[0] if len(parts) == 1 else jnp.concatenate(parts, axis=0)

    def count_ge(v):
        return cat([t.reduce(lambda b, _, vb=t.wide(v[sl]): jnp.where(b >= vb, 1.0, 0.0),
                             jnp.add, 0.0, jnp.sum) for sl, t in subs])

    los, his = [], []
    for _, t in subs:
        def group_max(c, acc, t=t):
            acc = list(acc)
            for b, _, par in t.blocks(c):
                acc[par] = jnp.maximum(acc[par], b)
            return tuple(acc)
        ga, gb = _loop(nchunks, group_max, (jnp.full((rsub, LANES), -inf, F32),) * 2)
        his.append(jnp.max(jnp.maximum(ga, gb), axis=1, keepdims=True))
        los.append(jnp.min(jnp.minimum(ga, gb), axis=1, keepdims=True))
    lo = jnp.maximum(_sortable(cat(los)) - 16, S_NEGINF)
    hi = jnp.minimum(_sortable(cat(his)), S_POSINF - 17) + 17
    done = jnp.where(all_vis, 1.0, 0.0)

    def step(st, coarse):
        lo, hi, nlo, done, it, _ = st
        mid = (lax.shift_right_arithmetic(lo, 1) + lax.shift_right_arithmetic(hi, 1) + (lo & hi & 1))
        if coarse:
            vmid = _sortable(0.5 * _unsortable(lo) + 0.5 * _unsortable(hi))
            vmid = jnp.minimum(jnp.maximum(vmid, lo + 1), hi - 1)
            far = jnp.where((lo ^ hi) < 0, 1, jnp.where(hi - lo > (1 << 25), 1, 0))
            mid = jnp.where(far > 0, vmid, mid)
            mid = jnp.where(lo == 0, jnp.where(hi > S_MIN_NORMAL, S_MIN_NORMAL, mid), mid)
            mid = jnp.where(lo < 0, jnp.where(hi > 0, 0, mid), mid)
        cnt = count_ge(_unsortable(mid))
        ge = cnt >= topk
        act = done < 0.5
        lo = jnp.where(act, jnp.where(ge, mid, lo), lo)
        nlo = jnp.where(act, jnp.where(ge, cnt, nlo), nlo)
        hi = jnp.where(act, jnp.where(ge, hi, mid), hi)
        zero_bracket = jnp.where(lo == 0, jnp.where(hi <= S_MIN_NORMAL, 1.0, 0.0), 0.0)
        done = jnp.maximum(done, jnp.where(nlo == topk, 1.0, jnp.where(hi <= lo + 1, 1.0, zero_bracket)))
        return lo, hi, nlo, done, it + 1, (jnp.min(done) < 0.5).astype(I32)

    state = (lo, hi, jnp.full((nrow, 1), -1.0, F32), done, jnp.int32(0), (jnp.min(done) < 0.5).astype(I32))
    state = lax.while_loop(lambda st: (st[5] > 0) & (st[4] < COARSE_PASSES),
                           functools.partial(step, coarse=True), state)
    lo, hi, nlo, done, _, _ = lax.while_loop(lambda st: (st[5] > 0) & (st[4] < 80),
                                             functools.partial(step, coarse=False), state)
    thr = _unsortable(jnp.where(all_vis, S_NEGINF + 1, lo))
    excess = jnp.where(all_vis, 0.0, nlo - topk)

    @pl.when(jnp.max(excess) > 0.0)
    def _():
        if nrow == SUBLANES:
            _drop_ties(_RowTools(s_sc, rows.start, SUBLANES, nchunks, cw), thr, excess, topk)
            return
        thr_ref, exc_ref = tie_scratch
        thr_ref[...] = jnp.broadcast_to(thr, (nrow, LANES))
        exc_ref[...] = jnp.broadcast_to(excess, (nrow, LANES))

        def group(g, _):
            r0 = pl.multiple_of(g * SUBLANES, SUBLANES)
            exc = exc_ref[pl.ds(r0, SUBLANES), :][:, :1]

            @pl.when(jnp.max(exc) > 0.0)
            def _():
                _drop_ties(_RowTools(s_sc, rows.start + r0, SUBLANES, nchunks, cw),
                           thr_ref[pl.ds(r0, SUBLANES), :][:, :1], exc, topk)
            return 0
        lax.fori_loop(0, nrow // SUBLANES, group, 0)

    return thr


class _RowTools:
    def __init__(self, s_sc, r0, nr, nchunks, cw):
        self.s_sc, self.rows, self.nr, self.nchunks, self.cw = s_sc, pl.ds(r0, nr), nr, nchunks, cw
        self.nl = cw // LANES
        self.lanef = lax.broadcasted_iota(I32, (nr, LANES), 1).astype(F32)

    def blocks(self, c):
        cw, nl = self.cw, self.nl
        off = c * cw if isinstance(c, int) else pl.multiple_of(c * cw, cw)
        blk = self.s_sc[self.rows, pl.ds(off, cw)]
        return [(blk[:, u * LANES:(u + 1) * LANES], off + u * LANES, (c * nl + u) % 2 if nl % 2 else u % 2)
                for u in range(nl)]

    def key_index(self, base):
        return self.lanef + (float(base) if isinstance(base, int) else base.astype(F32))

    def wide(self, v):
        return jnp.broadcast_to(v, (self.nr, LANES))

    def reduce(self, fn, comb, init, lane_red):
        def body(c, acc):
            for b, base, _ in self.blocks(c):
                acc = comb(acc, fn(b, base))
            return acc
        acc = _loop(self.nchunks, body, jnp.full((self.nr, LANES), init, F32))
        return lane_red(acc, axis=1, keepdims=True)

    def erase_where(self, drop_fn):
        def body(c, _):
            for b, base, _ in self.blocks(c):
                self.s_sc[self.rows, pl.ds(base, LANES)] = jnp.where(drop_fn(b, base), -float("inf"), b)
            return 0
        _loop(self.nchunks, body, 0)


def _drop_ties(t, thr, excess, topk):
    inf = float("inf")
    thrb = t.wide(thr)
    many = excess > MANY_TIES
    few = jnp.where(many, 0.0, excess)

    @pl.when(jnp.max(jnp.where(many, 1.0, 0.0)) > 0.0)
    def _():
        need = topk - t.reduce(lambda b, _: jnp.where(b > thrb, 1.0, 0.0), jnp.add, 0.0, jnp.sum)
        nbits = max(1, int(np.ceil(np.log2(t.s_sc.shape[1]))))

        def bis_idx(p, lo_i):
            cand = lo_i + lax.shift_left(jnp.int32(1), nbits - 1 - p)
            cb = t.wide(cand.astype(F32))
            f = t.reduce(lambda b, base: jnp.where((b == thrb) & (t.key_index(base) <= cb), 1.0, 0.0),
                         jnp.add, 0.0, jnp.sum)
            return jnp.where(f < need, cand, lo_i)
        last = (lax.fori_loop(0, nbits, bis_idx, jnp.full(thr.shape, -1, I32)) + 1).astype(F32)
        lb = t.wide(jnp.where(many, last, float(2 ** 30)))
        t.erase_where(lambda b, base: (b == thrb) & (t.key_index(base) > lb))

    nmax = jnp.max(few)

    @pl.when(nmax > 0.0)
    def _():
        def drop(it, _):
            vminb = t.wide(t.reduce(lambda b, _: jnp.where(b >= thrb, b, inf), jnp.minimum, inf, jnp.min))
            last = t.reduce(lambda b, base: jnp.where(b == vminb, t.key_index(base), -1.0),
                            jnp.maximum, -1.0, jnp.max)
            tgtb = t.wide(jnp.where(few > it.astype(F32), last, -2.0))
            t.erase_where(lambda b, base: t.key_index(base) == tgtb)
            return 0
        lax.fori_loop(0, nmax.astype(I32), drop, 0)


def _prompt_select_kernel(qic_ref, wi_ref, kic_ref, bias_ref, s_sc, thr_sc, tie_sc, wb_sc, *, tq, cs, topk):
    q0 = pl.program_id(0) * tq
    nvis = q0 + tq
    nch = (nvis + cs - 1) // cs
    nl = cs // LANES
    for h in range(IDX_HEADS):
        wb_sc[h] = jnp.broadcast_to(wi_ref[:, h:h + 1], (tq, LANES))
    qpos = q0 + lax.broadcasted_iota(I32, (tq, cs), 0)
    kiota = lax.broadcasted_iota(I32, (tq, cs), 1)

    def score_chunk(c, _):
        off = pl.multiple_of(c * cs, cs)
        kc = kic_ref[pl.ds(off, cs), :]
        acc = jnp.zeros((tq, cs), F32)
        for h in range(IDX_HEADS):
            s = _dot_nt(qic_ref[:, h * 256:(h + 1) * 256], kc)
            acc = acc + jnp.maximum(s, 0.0) * _tile_lanes(wb_sc[h], nl)
        s_sc[:, pl.ds(off, cs)] = jnp.where(kiota + off <= qpos, acc, -float("inf"))
        return 0
    lax.fori_loop(0, nch, score_chunk, 0)

    wide = 2 if s_sc.shape[1] % (2 * cs) == 0 else 1
    npad = (-nch) % wide

    @pl.when(npad > 0)
    def _():
        s_sc[:, pl.ds(pl.multiple_of(nch * cs, cs), cs)] = jnp.full((tq, cs), -float("inf"), F32)

    all_vis = q0 + lax.broadcasted_iota(I32, (tq, 1), 0) < topk
    thr = _select_rows(s_sc, slice(0, tq), (nch + npad) // wide, wide * cs, topk, all_vis,
                       rsub=min(128, tq), tie_scratch=(thr_sc, tie_sc))
    thrb = jnp.broadcast_to(thr, (tq, cs))

    def emit(c, _):
        sl = pl.ds(pl.multiple_of(c * cs, cs), cs)
        bias_ref[:, sl] = jnp.where(s_sc[:, sl] >= thrb, 0.0, NEG).astype(BF16)
        return 0
    lax.fori_loop(0, nch, emit, 0)

    def fill(c, _):
        bias_ref[:, pl.ds(pl.multiple_of(c * cs, cs), cs)] = jnp.full((tq, cs), NEG, BF16)
        return 0
    lax.fori_loop(nch, bias_ref.shape[1] // cs, fill, 0)


def _prompt_select(qic, wi, kic, *, topk, tq, cs):
    t = kic.shape[0]
    qrow = lambda w: pl.BlockSpec((tq, w), lambda i: (i, 0))
    return pl.pallas_call(
        functools.partial(_prompt_select_kernel, tq=tq, cs=cs, topk=topk),
        grid=(t // tq,),
        in_specs=[qrow(qic.shape[1]), qrow(LANES), pl.BlockSpec(kic.shape, lambda i: (0, 0))],
        out_specs=qrow(t),
        out_shape=jax.ShapeDtypeStruct((t, t), BF16),
        scratch_shapes=[
            pltpu.VMEM((tq, t), F32),
            pltpu.VMEM((tq, LANES), F32),
            pltpu.VMEM((tq, LANES), F32),
            pltpu.VMEM((IDX_HEADS, tq, LANES), F32),
        ],
        compiler_params=pltpu.CompilerParams(dimension_semantics=("arbitrary",),
                                             vmem_limit_bytes=VMEM_LIMIT),
        name="prompt_select",
    )(qic, wi, kic)


def _prompt_attn_kernel(qm_ref, kt_ref, vb_ref, bias_ref, o_ref, lg_sc, p_sc, alpha_sc, m_sc, l_sc, acc_sc,
                        *, tq, ts, cs, rb, n_heads):
    i = pl.program_id(0)
    j = pl.program_id(1)
    nkt = pl.num_programs(1)
    nvis = (i + 1) * tq
    nl = cs // LANES

    @pl.when(j == 0)
    def _():
        m_sc[...] = jnp.full(m_sc.shape, NEG, F32)
        l_sc[...] = jnp.zeros(l_sc.shape, F32)
        acc_sc[...] = jnp.zeros(acc_sc.shape, F32)

    @pl.when(j * ts < nvis)
    def _():
        nsub = jnp.minimum((nvis - j * ts + cs - 1) // cs, ts // cs)
        lane = lax.broadcasted_iota(I32, (tq, LANES), 1)
        lo_half = lane < HEAD_DIM

        def sub(u, _):
            loc = pl.multiple_of(u * cs, cs)
            for h in range(n_heads):
                pr = h // 2
                sl = h % 2
                kt = kt_ref[pr * LANES:(pr + 1) * LANES, pl.ds(loc, cs)]
                lg_sc[sl] = _dot(qm_ref[:, h * LANES:(h + 1) * LANES], kt).astype(BF16)
                for r in range(tq // rb):
                    rows = slice(r * rb, (r + 1) * rb)
                    x = lg_sc[sl, rows, :] + bias_ref[rows, pl.ds(loc, cs)]
                    mx = x[:, :LANES]
                    for u2 in range(1, nl):
                        mx = jnp.maximum(mx, x[:, u2 * LANES:(u2 + 1) * LANES])
                    m_prev = m_sc[h, rows, :]
                    m_new = jnp.maximum(m_prev, jnp.max(mx.astype(F32), axis=1, keepdims=True))
                    p = jnp.exp2(x - _tile_lanes(m_new.astype(BF16), nl))
                    ps = p[:, :LANES]
                    for u2 in range(1, nl):
                        ps = ps + p[:, u2 * LANES:(u2 + 1) * LANES]
                    alpha = jnp.exp2(m_prev - m_new)
                    l_sc[h, rows, :] = (alpha * l_sc[h, rows, :]
                                        + jnp.sum(ps.astype(F32), axis=1, keepdims=True))
                    m_sc[h, rows, :] = m_new
                    alpha_sc[sl, rows, :] = alpha
                    p_sc[sl, rows, :] = p
                vt = vb_ref[pl.ds(loc, cs), pr * LANES:(pr + 1) * LANES]
                pv = _dot(p_sc[sl], vt)
                a = acc_sc[pr]
                mine = lo_half if h % 2 == 0 else jnp.logical_not(lo_half)
                acc_sc[pr] = jnp.where(mine, alpha_sc[sl] * a + pv, a)
            return 0
        lax.fori_loop(0, nsub, sub, 0)

    @pl.when(j == nkt - 1)
    def _():
        lane = lax.broadcasted_iota(I32, (tq, LANES), 1)
        lo_half = lane < HEAD_DIM
        for pr in range(n_heads // 2):
            l = jnp.where(lo_half, l_sc[2 * pr], l_sc[2 * pr + 1])
            o_ref[:, pr * LANES:(pr + 1) * LANES] = acc_sc[pr] / l


def _prompt_attn(qm, kt, vb, bias, *, tq, ts, cs):
    t, d_attn = vb.shape
    n_heads = d_attn // HEAD_DIM
    nq, nkt = t // tq, t // ts
    last = lambda i, j: jnp.minimum(j, ((i + 1) * tq - 1) // ts)
    qrow = lambda w: pl.BlockSpec((tq, w), lambda i, j: (i, 0))
    return pl.pallas_call(
        functools.partial(_prompt_attn_kernel, tq=tq, ts=ts, cs=cs, rb=min(32, tq), n_heads=n_heads),
        grid=(nq, nkt),
        in_specs=[qrow(qm.shape[1]),
                  pl.BlockSpec((d_attn, ts), lambda i, j: (0, last(i, j))),
                  pl.BlockSpec((ts, d_attn), lambda i, j: (last(i, j), 0)),
                  pl.BlockSpec((tq, ts), lambda i, j: (i, last(i, j)))],
        out_specs=qrow(d_attn),
        out_shape=jax.ShapeDtypeStruct((t, d_attn), F32),
        scratch_shapes=[
            pltpu.VMEM((2, tq, cs), BF16),
            pltpu.VMEM((2, tq, cs), BF16),
            pltpu.VMEM((2, tq, LANES), F32),
            pltpu.VMEM((n_heads, tq, LANES), F32),
            pltpu.VMEM((n_heads, tq, LANES), F32),
            pltpu.VMEM((n_heads // 2, tq, LANES), F32),
        ],
        compiler_params=pltpu.CompilerParams(dimension_semantics=("arbitrary", "arbitrary"),
                                             vmem_limit_bytes=VMEM_LIMIT),
        name="prompt_attn",
    )(qm, kt, vb, bias)


def _sample_sel_kernel(pt_ref, *refs, ppg, past, topk):
    pages = refs[:ppg]
    qhl_ref, wexp_ref, knh_ref, knl_ref, bias_ref, s_sc = refs[ppg:]
    g = pl.program_id(1)
    ng = pl.num_programs(1)
    nq = bias_ref.shape[0]
    kp = bias_ref.shape[1]
    qhl = qhl_ref[...]
    nr = qhl.shape[0] // 2

    def scores(kh, kl, mm):
        both = mm(qhl, kh)
        s = both[:nr] + both[nr:] + mm(qhl[:nr], kl)
        s = jnp.maximum(s, 0.0) * _tile_lanes(wexp_ref[...], s.shape[1] // PAGE)
        acc = s[0:nq]
        for h in range(1, IDX_HEADS):
            acc = acc + s[h * nq:(h + 1) * nq]
        return acc

    kh, kl = _split_bf16(jnp.concatenate([pages[r][...] for r in range(ppg)], axis=1))
    off = pl.multiple_of(g * (ppg * PAGE), ppg * PAGE)
    s_sc[:, pl.ds(off, ppg * PAGE)] = scores(kh, kl, _dot)

    @pl.when(g == ng - 1)
    def _():
        sn = scores(knh_ref[...], knl_ref[...], _dot_nt)
        qi = lax.broadcasted_iota(I32, (nq, PAGE), 0)
        s_sc[:, past:] = jnp.where(lax.broadcasted_iota(I32, (nq, PAGE), 1) <= qi, sn, -float("inf"))
        all_vis = past + lax.broadcasted_iota(I32, (nq, 1), 0) < topk
        thrb = jnp.broadcast_to(
            _select_rows(s_sc, slice(0, nq), kp // PAGE, PAGE, topk, all_vis, rsub=nq), (nq, LANES))
        for c in range(kp // PAGE):
            sl = slice(c * PAGE, (c + 1) * PAGE)
            bias_ref[:, sl] = jnp.where(s_sc[:, sl] >= thrb, 0.0, NEG)


def _sample_sel(page_table, cache_kidx_t, layer, qhl, wexp, knh, knl, *, topk, ppg):
    nb, npages = page_table.shape
    past = npages * PAGE
    kp = past + PAGE
    nq = qhl.shape[1] // (2 * IDX_HEADS)
    ng = npages // ppg

    def page_spec(r):
        return pl.BlockSpec((None, None, IDX_DIM, PAGE),
                            lambda b, g, pt: (layer, pt[b * npages + g * ppg + r], 0, 0))
    per_seq = lambda a: pl.BlockSpec((None,) + a.shape[1:], lambda b, g, pt: (b, 0, 0))
    return pl.pallas_call(
        functools.partial(_sample_sel_kernel, ppg=ppg, past=past, topk=topk),
        grid_spec=pltpu.PrefetchScalarGridSpec(
            num_scalar_prefetch=1,
            grid=(nb, ng),
            in_specs=[page_spec(r) for r in range(ppg)] + [per_seq(a) for a in (qhl, wexp, knh, knl)],
            out_specs=pl.BlockSpec((None, nq, kp), lambda b, g, pt: (b, 0, 0)),
            scratch_shapes=[pltpu.VMEM((nq, kp), F32)],
        ),
        out_shape=jax.ShapeDtypeStruct((nb, nq, kp), F32),
        compiler_params=pltpu.CompilerParams(dimension_semantics=("arbitrary", "arbitrary"),
                                             vmem_limit_bytes=VMEM_LIMIT),
        name="sample_select",
    )(page_table.reshape(-1), *([cache_kidx_t] * ppg), qhl, wexp, knh, knl)


def _sample_attn_kernel(pt_ref, *refs, ppg, n_heads):
    kpages = refs[:ppg]
    vpages = refs[ppg:2 * ppg]
    qbd_ref, bias_ref, biasn_ref, kn_ref, vn_ref, o_ref, m_sc, l_sc, acc_sc = refs[2 * ppg:]
    g = pl.program_id(1)
    ng = pl.num_programs(1)
    nq = o_ref.shape[0]
    d_attn = o_ref.shape[1]
    qbd = qbd_ref[...]

    def heads(b):
        return jnp.concatenate([b] * n_heads, axis=0)

    def update(lg, pv_of):
        m_prev = m_sc[...]
        m_new = jnp.maximum(m_prev, jnp.max(lg, axis=1, keepdims=True))
        alpha = jnp.exp2(m_prev - m_new)
        p = jnp.exp2(lg - m_new[:, :1])
        l_sc[...] = alpha * l_sc[...] + jnp.sum(p, axis=1, keepdims=True)
        m_sc[...] = m_new
        acc_sc[...] = alpha[:, :1] * acc_sc[...] + pv_of(p.astype(BF16))

    @pl.when(g == 0)
    def _():
        m_sc[...] = jnp.full(m_sc.shape, NEG, F32)
        l_sc[...] = jnp.zeros(l_sc.shape, F32)
        acc_sc[...] = jnp.zeros(acc_sc.shape, F32)
        update(_dot_nt(qbd, kn_ref[...]) + heads(biasn_ref[...]), lambda p: _dot(p, vn_ref[...]))

    lg = jnp.concatenate(
        [_dot(qbd, kpages[r][...].reshape(d_attn, PAGE).astype(BF16)) for r in range(ppg)], axis=1)

    def pv_pages(p):
        pv = _dot_nt(p[:, :PAGE], vpages[0][...].reshape(d_attn, PAGE).astype(BF16))
        for r in range(1, ppg):
            pv = pv + _dot_nt(p[:, r * PAGE:(r + 1) * PAGE],
                              vpages[r][...].reshape(d_attn, PAGE).astype(BF16))
        return pv
    update(lg + heads(bias_ref[...]), pv_pages)

    @pl.when(g == ng - 1)
    def _():
        res = acc_sc[...] / l_sc[:, :1]
        lane = lax.broadcasted_iota(I32, (nq, d_attn), 1)
        out = jnp.zeros((nq, d_attn), F32)
        for h in range(n_heads):
            mine = (lane >= h * HEAD_DIM) & (lane < (h + 1) * HEAD_DIM)
            out = jnp.where(mine, res[h * nq:(h + 1) * nq], out)
        o_ref[...] = out


def _sample_attn(page_table, cache_k_t, cache_v_t, layer, qbd, bias, kn, vn, *, ppg):
    nb, npages = page_table.shape
    n_heads = cache_k_t.shape[2]
    d_attn = n_heads * HEAD_DIM
    nq = bias.shape[1]
    ng = npages // ppg

    def page_spec(r):
        return pl.BlockSpec((None, None, n_heads, HEAD_DIM, PAGE),
                            lambda b, g, pt: (layer, pt[b * npages + g * ppg + r], 0, 0, 0))
    per_seq = lambda a: pl.BlockSpec((None,) + a.shape[1:], lambda b, g, pt: (b, 0, 0))
    return pl.pallas_call(
        functools.partial(_sample_attn_kernel, ppg=ppg, n_heads=n_heads),
        grid_spec=pltpu.PrefetchScalarGridSpec(
            num_scalar_prefetch=1,
            grid=(nb, ng),
            in_specs=([page_spec(r) for r in range(ppg)] * 2 + [
                per_seq(qbd),
                pl.BlockSpec((None, nq, ppg * PAGE), lambda b, g, pt: (b, 0, g)),
                pl.BlockSpec((None, nq, PAGE), lambda b, g, pt: (b, 0, npages)),
                per_seq(kn), per_seq(vn)]),
            out_specs=pl.BlockSpec((None, nq, d_attn), lambda b, g, pt: (b, 0, 0)),
            scratch_shapes=[pltpu.VMEM((n_heads * nq, LANES), F32),
                            pltpu.VMEM((n_heads * nq, LANES), F32),
                            pltpu.VMEM((n_heads * nq, d_attn), F32)],
        ),
        out_shape=jax.ShapeDtypeStruct((nb, nq, d_attn), F32),
        compiler_params=pltpu.CompilerParams(dimension_semantics=("arbitrary", "arbitrary"),
                                             vmem_limit_bytes=VMEM_LIMIT),
        name="sample_attn",
    )(page_table.reshape(-1), *([cache_k_t] * ppg), *([cache_v_t] * ppg), qbd, bias, bias, kn, vn)


def _mix_out(y, gc, attn, ga, x, lng, lnb, wo_ref, gpost, d_conv):
    mu = jnp.mean(y, axis=-1, keepdims=True)
    yc = y - mu
    var = jnp.mean(yc * yc, axis=-1, keepdims=True)
    yn = yc * lax.rsqrt(var + EPS) * lng + lnb
    conv_out = _silu(yn) * gc
    o = (_dot(conv_out.astype(BF16), wo_ref[:d_conv, :])
         + _dot((attn * ga).astype(BF16), wo_ref[d_conv:, :]))
    ms = jnp.mean(o * o, axis=-1, keepdims=True)
    return x + o * lax.rsqrt(ms + EPS) * gpost


def _finish_prompt_kernel(glu_ref, halo_ref, gc_ref, attn_ref, ga_ref, x_ref, wdw_ref, bdw_ref,
                          lng_ref, lnb_ref, wo_ref, gpost_ref, y_ref, win_sc, shift_sc, conv_sc, *, rb):
    i = pl.program_id(0)
    tm, d_conv = glu_ref.shape
    halo = halo_ref[...]
    win_sc[:HALO, :] = jnp.where(i == 0, jnp.zeros_like(halo), halo)
    win_sc[HALO:, :] = glu_ref[...]
    span = shift_sc.shape[1]
    for p in range(1, SUBLANES):
        shift_sc[p - 1] = win_sc[p:p + span, :]
    base = HALO - (CONV_WIDTH - 1)
    for r in range(tm // rb):
        acc = jnp.broadcast_to(bdw_ref[...], (rb, d_conv))
        for t in range(CONV_WIDTH):
            p = (base + t) % SUBLANES
            s = base + t - p + r * rb
            rows = win_sc[s:s + rb, :] if p == 0 else shift_sc[p - 1, s:s + rb, :]
            acc = acc + wdw_ref[t:t + 1, :] * rows
        conv_sc[r * rb:(r + 1) * rb, :] = acc
    y_ref[...] = _mix_out(conv_sc[...], gc_ref[...], attn_ref[...], ga_ref[...], x_ref[...],
                          lng_ref[...], lnb_ref[...], wo_ref, gpost_ref[...], d_conv)


def _finish_prompt(glu, gc, attn, ga, x, w_dw, b_dw, ln_g, ln_b, w_out, g_post, *, tm):
    t, d_conv = glu.shape
    d_model = x.shape[1]
    row = lambda w: pl.BlockSpec((tm, w), lambda i: (i, 0))
    full = lambda a: pl.BlockSpec(a.shape, lambda i: (0, 0))
    return pl.pallas_call(
        functools.partial(_finish_prompt_kernel, rb=32),
        grid=(t // tm,),
        in_specs=[row(d_conv),
                  pl.BlockSpec((HALO, d_conv), lambda i: (jnp.maximum(i * (tm // HALO) - 1, 0), 0)),
                  row(d_conv), row(attn.shape[1]), row(ga.shape[1]), row(d_model),
                  full(w_dw), full(b_dw), full(ln_g), full(ln_b), full(w_out), full(g_post)],
        out_specs=row(d_model),
        out_shape=jax.ShapeDtypeStruct((t, d_model), F32),
        scratch_shapes=[pltpu.VMEM((tm + HALO, d_conv), F32),
                        pltpu.VMEM((SUBLANES - 1, tm + HALO - SUBLANES, d_conv), F32),
                        pltpu.VMEM((tm, d_conv), F32)],
        compiler_params=pltpu.CompilerParams(dimension_semantics=("arbitrary",),
                                             vmem_limit_bytes=VMEM_LIMIT),
        name="finish_prompt",
    )(glu, glu, gc, attn, ga, x, w_dw, b_dw, ln_g, ln_b, w_out, g_post)


def _finish_sample_kernel(hist_ref, gc_ref, attn_ref, ga_ref, x_ref, wdw_ref, bdw_ref,
                          lng_ref, lnb_ref, wo_ref, gpost_ref, y_ref, *, nq):
    nb, _, d_conv = hist_ref.shape
    acc = jnp.broadcast_to(bdw_ref[...].reshape(1, 1, d_conv), (nb, nq, d_conv))
    for t in range(CONV_WIDTH):
        acc = acc + wdw_ref[t:t + 1, :].reshape(1, 1, d_conv) * hist_ref[:, t:t + nq, :]
    y_ref[...] = _mix_out(acc.reshape(nb * nq, d_conv), gc_ref[...], attn_ref[...], ga_ref[...],
                          x_ref[...], lng_ref[...], lnb_ref[...], wo_ref, gpost_ref[...], d_conv)


def _finish_sample(hist, gc, attn, ga, x, w_dw, b_dw, ln_g, ln_b, w_out, g_post, *, nq):
    args = (hist, gc, attn, ga, x, w_dw, b_dw, ln_g, ln_b, w_out, g_post)
    full = lambda a: pl.BlockSpec(a.shape, lambda i: (0,) * a.ndim)
    return pl.pallas_call(
        functools.partial(_finish_sample_kernel, nq=nq),
        grid=(1,),
        in_specs=[full(a) for a in args],
        out_specs=full(x),
        out_shape=jax.ShapeDtypeStruct(x.shape, F32),
        compiler_params=pltpu.CompilerParams(dimension_semantics=("arbitrary",),
                                             vmem_limit_bytes=VMEM_LIMIT),
        name="finish_sample",
    )(*args)


def _tiles(seq):
    tq = min(256, seq)
    tqa = min(1024, seq)
    cs = min(512, seq)
    ts = min(2048, seq)
    return tq, tqa, ts, cs


def _layer(layer, xp, xs, cache_k, cache_v, cache_kidx, state, page_table, g_pre, w_in, w_dw, b_dw,
           ln_g, ln_b, w_out, g_post, *, past_len):
    seq, d_model = xp.shape
    nb, nq, _ = xs.shape
    d_mix = w_out.shape[0]
    d_conv = d_mix // 2
    d_attn = d_mix - d_conv
    n_heads = d_attn // HEAD_DIM
    nqi = IDX_HEADS * IDX_DIM
    idx_scale = float(nqi) ** -0.5
    n_main = 3 * d_conv + 4 * d_attn

    w_main = w_in[:, :n_main].astype(BF16)
    zcol = lambda n: jnp.zeros((d_model, n), F32)
    w_idx = jnp.concatenate([w_in[:, n_main:n_main + nqi + IDX_DIM], zcol(LANES - IDX_DIM),
                             w_in[:, n_main + nqi + IDX_DIM:], zcol(LANES - IDX_HEADS)], axis=1)
    w_idx_hi = w_idx.astype(BF16)
    w_idx_lo = (w_idx - w_idx_hi.astype(F32)).astype(BF16)
    w_out_b = w_out.astype(BF16)
    w_dw_p = jnp.concatenate([w_dw, jnp.zeros((1, d_conv), F32)], axis=0)
    row = lambda a: a.reshape(1, -1)
    proj = functools.partial(_project, g_pre=row(g_pre), w_main=w_main, w_idx_hi=w_idx_hi,
                             w_idx_lo=w_idx_lo, d_conv=d_conv, d_attn=d_attn, idx_scale=idx_scale)
    fin_w = (w_dw_p, row(b_dw), row(ln_g), row(ln_b), w_out_b, row(g_post))

    tq, tqa, ts, cs = _tiles(seq)
    (glu, gc, k32, v32, kidx, qm, _, vb, ga, qic, kic, wi, kt) = proj(xp, jnp.arange(seq), tm=min(256, seq))
    bias_p = _prompt_select(qic, wi, kic, topk=min(TOPK_MAX, seq // 4), tq=tq, cs=cs)
    attn = _prompt_attn(qm, kt, vb, bias_p, tq=tqa, ts=ts, cs=cs)
    yp = _finish_prompt(glu, gc, attn, ga, xp, *fin_w, tm=min(256, seq))
    conv_p = glu[seq - (CONV_WIDTH - 1):]

    xs2 = xs.reshape(nb * nq, d_model)
    pos_s = past_len + jnp.tile(jnp.arange(nq), nb)
    (glu_s, gc_s, k32_s, v32_s, kidx_s, qm_s, kb_s, vb_s, ga_s, qic_s, kic_s, wi_s, _) = proj(
        xs2, pos_s, tm=nb * nq)
    by_head = lambda a, nh: a.reshape(nb, nq, nh, -1).transpose(0, 2, 1, 3)
    qic4 = by_head(qic_s, IDX_HEADS).reshape(nb, IDX_HEADS * nq, 256)
    qhl = jnp.concatenate([qic4[..., :IDX_DIM], qic4[..., LANES:LANES + IDX_DIM]], axis=1)
    wexp = jnp.broadcast_to(by_head(wi_s[:, :IDX_HEADS], IDX_HEADS).reshape(nb, IDX_HEADS * nq, 1),
                            (nb, IDX_HEADS * nq, PAGE))
    pad_keys = lambda a: jnp.pad(a, [(0, 0)] * (a.ndim - 2) + [(0, PAGE - nq), (0, 0)])
    knh = pad_keys(kic_s[:, :IDX_DIM].reshape(nb, nq, IDX_DIM))
    knl = pad_keys(kic_s[:, IDX_DIM:2 * IDX_DIM].reshape(nb, nq, IDX_DIM))
    topk_s = min(TOPK_MAX, (past_len + nq) // 4)
    ppg = math.gcd(page_table.shape[1], PAGES_PER_STEP)
    bias = _sample_sel(page_table, cache_kidx.transpose(0, 1, 3, 2), layer, qhl, wexp, knh, knl,
                       topk=topk_s, ppg=ppg)
    onehot = (jnp.arange(n_heads)[:, None] // 2 == jnp.arange(n_heads // 2)[None, :]).astype(BF16)
    qbd = (by_head(qm_s, n_heads)[:, :, :, None, :]
           * onehot[None, :, None, :, None]).reshape(nb, n_heads * nq, d_attn)
    attn_s = _sample_attn(page_table, cache_k.transpose(0, 1, 3, 4, 2), cache_v.transpose(0, 1, 3, 4, 2),
                          layer, qbd, bias, pad_keys(kb_s.reshape(nb, nq, d_attn)),
                          pad_keys(vb_s.reshape(nb, nq, d_attn)), ppg=ppg).reshape(nb * nq, d_attn)
    hist_s = jnp.concatenate([state, glu_s.reshape(nb, nq, d_conv),
                              jnp.zeros((nb, 2, d_conv), F32)], axis=1)
    ys = _finish_sample(hist_s, gc_s, attn_s, ga_s, xs2, *fin_w, nq=nq)
    conv_s = hist_s[:, nq:nq + CONV_WIDTH - 1]

    hd = lambda a, lead: a.reshape(lead + (n_heads, HEAD_DIM))
    return (yp, ys.reshape(nb, nq, d_model),
            hd(k32, (1, seq)), hd(v32, (1, seq)), kidx.reshape(1, seq, IDX_DIM), conv_p[None],
            hd(k32_s, (nb, nq)), hd(v32_s, (nb, nq)), kidx_s.reshape(nb, nq, IDX_DIM), conv_s)


def kernel(x_prompt, x_sample, cache_k, cache_v, cache_kidx, state_conv, page_table, g_pre, w_in, w_dw,
           b_dw, ln_g, ln_b, w_out, g_post):
    depth = w_in.shape[0]
    assert x_prompt.shape[0] == 1, "one prompt sequence per call"
    past_len = page_table.shape[1] * PAGE
    xp, xs = x_prompt[0], x_sample
    outs = [[] for _ in range(8)]
    for l in range(depth):
        res = _layer(l, xp, xs, cache_k, cache_v, cache_kidx, state_conv[l], page_table,
                     g_pre[l], w_in[l], w_dw[l], b_dw[l], ln_g[l], ln_b[l], w_out[l], g_post[l],
                     past_len=past_len)
        xp, xs = res[0], res[1]
        for o, r in zip(outs, res[2:]):
            o.append(r)
    return (xp[None], xs) + tuple(jnp.stack(o) for o in outs)
```
